```python
import jax, jax.numpy as jnp
from jax import lax
import numpy as np

D_MODEL = 1024
BATCH = 2
SEQ = 16384
DEPTH = 2

D_CONV = 1024
CONV_WIDTH = 3
N_HEADS = 8
HEAD_DIM = 128
D_ATT = N_HEADS * HEAD_DIM
N_IDX_HEADS = 8
IDX_DIM = 64
MAX_TOPK = 256
Q_BLOCK = 128
N_MEM = 256
N_MEM_HEADS = 4
MEM_HEAD_DIM = 256
D_MEM = N_MEM_HEADS * MEM_HEAD_DIM
N_BRANCH = 3
D_FF = 3584
N_EXPERTS = 8
TOP_K = 2
MOE_BLOCK = 256
N_DENSE = (DEPTH + 1) // 2
N_MOE = DEPTH // 2
ROPE_THETA = 10000.0
EPS = 1e-6

SPLITS = [D_CONV, D_CONV, D_CONV, D_ATT, D_ATT, D_ATT, N_IDX_HEADS * IDX_DIM, IDX_DIM, N_IDX_HEADS, D_MEM, D_MODEL, D_MODEL, D_MODEL]
D_IN = 3 * D_CONV + 3 * D_ATT + N_IDX_HEADS * IDX_DIM + IDX_DIM + N_IDX_HEADS + D_MEM + N_BRANCH * D_MODEL

kernel_name = 'hybrid_gatedconv_dsa_memxattn_moe'


def rms_norm(x, g):
    xf = x.astype(jnp.float32)
    y = xf * lax.rsqrt(jnp.mean(xf * xf, axis=-1, keepdims=True) + EPS)
    return (y * g.astype(jnp.float32)).astype(x.dtype)


def rope_tables(seq, dim):
    inv = 1.0 / (ROPE_THETA ** (jnp.arange(0, dim, 2, dtype=jnp.float32) / dim))
    ang = jnp.arange(seq, dtype=jnp.float32)[:, None] * inv[None, :]
    return jnp.cos(ang), jnp.sin(ang)


def apply_rope(x, cos, sin):
    c = cos[None, :, None, :].astype(x.dtype)
    s = sin[None, :, None, :].astype(x.dtype)
    x1, x2 = jnp.split(x, 2, axis=-1)
    return jnp.concatenate([x1 * c - x2 * s, x2 * c + x1 * s], axis=-1)


def causal_dwconv(u, w):
    return lax.conv_general_dilated(u, w[:, None, :], window_strides=(1,), padding=[(CONV_WIDTH - 1, 0)], dimension_numbers=('NWC', 'WIO', 'NWC'), feature_group_count=u.shape[-1])


def swiglu(x, w_gate, w_up, w_down):
    return jnp.matmul(jax.nn.silu(jnp.matmul(x, w_gate)) * jnp.matmul(x, w_up), w_down)


def dsa_attention(q, k, v, qi, ki, wi, topk):
    b, s, h, dh = q.shape
    n_blk = s // Q_BLOCK
    key_pos = jnp.arange(s)
    scale = dh ** -0.5

    def block(i):
        start = i * Q_BLOCK
        qb = lax.dynamic_slice_in_dim(q, start, Q_BLOCK, axis=1)
        qib = lax.dynamic_slice_in_dim(qi, start, Q_BLOCK, axis=1)
        wib = lax.dynamic_slice_in_dim(wi, start, Q_BLOCK, axis=1)
        q_pos = start + jnp.arange(Q_BLOCK)
        rel = jax.nn.relu(jnp.einsum('bqhd,bsd->bqhs', qib, ki).astype(jnp.float32))
        score = jnp.einsum('bqhs,bqh->bqs', rel, wib.astype(jnp.float32))
        causal = key_pos[None, :] <= q_pos[:, None]
        score = jnp.where(causal[None], score, -jnp.inf)
        _, idx = lax.top_k(score, topk)
        k_sel = jax.vmap(lambda kk, ii: kk[ii])(k, idx)
        v_sel = jax.vmap(lambda vv, ii: vv[ii])(v, idx)
        logits = jnp.einsum('bqhd,bqkhd->bhqk', qb, k_sel).astype(jnp.float32) * scale
        valid = idx <= q_pos[None, :, None]
        logits = jnp.where(valid[:, None], logits, -jnp.inf)
        p = jax.nn.softmax(logits, axis=-1).astype(v.dtype)
        return jnp.einsum('bhqk,bqkhd->bqhd', p, v_sel)

    out = lax.map(block, jnp.arange(n_blk))
    return jnp.moveaxis(out, 0, 1).reshape(b, s, h * dh)


def memory_cross_attention(qm, mem, mem_norm, w_mem_kv, mq_norm, mk_norm):
    b, s, _ = qm.shape
    m = rms_norm(mem, mem_norm)
    mk, mv = jnp.split(jnp.matmul(m, w_mem_kv), 2, axis=-1)
    n_mem = mem.shape[1]
    mk = rms_norm(mk.reshape(b, n_mem, N_MEM_HEADS, MEM_HEAD_DIM), mk_norm)
    mv = mv.reshape(b, n_mem, N_MEM_HEADS, MEM_HEAD_DIM)
    qh = rms_norm(qm.reshape(b, s, N_MEM_HEADS, MEM_HEAD_DIM), mq_norm)
    logits = jnp.einsum('bshd,bmhd->bhsm', qh, mk).astype(jnp.float32) * (MEM_HEAD_DIM ** -0.5)
    p = jax.nn.softmax(logits, axis=-1).astype(mv.dtype)
    return jnp.einsum('bhsm,bmhd->bshd', p, mv).reshape(b, s, D_MEM)


def mixer_block(x, mem, attn_norm, w_in, conv_w, q_norm, k_norm, mem_norm, w_mem_kv, mq_norm, mk_norm, w_br_conv, w_br_attn, w_br_mem, w_out, rope_att, rope_idx, topk):
    b, s, _ = x.shape
    h = rms_norm(x, attn_norm)
    offsets = [int(o) for o in np.cumsum(SPLITS)[:-1]]
    (cb, cc, cu, q, k, v, qi, ki, wi, qm, g_conv, g_att, g_mem) = jnp.split(jnp.matmul(h, w_in), offsets, axis=-1)

    conv_out = cb * causal_dwconv(cc * cu, conv_w)

    cos, sin = rope_att
    cos_i, sin_i = rope_idx
    q = apply_rope(rms_norm(q.reshape(b, s, N_HEADS, HEAD_DIM), q_norm), cos, sin)
    k = apply_rope(rms_norm(k.reshape(b, s, N_HEADS, HEAD_DIM), k_norm), cos, sin)
    v = v.reshape(b, s, N_HEADS, HEAD_DIM)
    qi = apply_rope(qi.reshape(b, s, N_IDX_HEADS, IDX_DIM), cos_i, sin_i)
    ki = apply_rope(ki.reshape(b, s, 1, IDX_DIM), cos_i, sin_i)[:, :, 0]
    wi = wi * ((N_IDX_HEADS * IDX_DIM) ** -0.5)
    att_out = dsa_attention(q, k, v, qi, ki, wi, topk)

    mem_out = memory_cross_attention(qm, mem, mem_norm, w_mem_kv, mq_norm, mk_norm)

    merged = (jax.nn.sigmoid(g_conv) * jnp.matmul(conv_out, w_br_conv)
              + jax.nn.sigmoid(g_att) * jnp.matmul(att_out, w_br_attn)
              + jax.nn.sigmoid(g_mem) * jnp.matmul(mem_out, w_br_mem))
    return jnp.matmul(merged, w_out)


def moe_swiglu(h, router, w_gate, w_up, w_down):
    b, s, d = h.shape
    n_tok = b * s
    n_asg = n_tok * TOP_K
    t = h.reshape(n_tok, d)
    logits = jnp.matmul(t, router).astype(jnp.float32)
    top_val, top_idx = lax.top_k(logits, TOP_K)
    gate = jax.nn.softmax(top_val, axis=-1)
    eid = top_idx.reshape(-1)
    tok = jnp.repeat(jnp.arange(n_tok, dtype=jnp.int32), TOP_K)
    gw = gate.reshape(-1)
    order = jnp.argsort(eid)
    eid_s, tok_s, gw_s = eid[order], tok[order], gw[order]
    counts = jnp.bincount(eid, length=N_EXPERTS)
    padded = (counts + MOE_BLOCK - 1) // MOE_BLOCK * MOE_BLOCK
    pad_end = jnp.cumsum(padded)
    pad_start = pad_end - padded
    seg_start = jnp.cumsum(counts) - counts
    dest = pad_start[eid_s] + jnp.arange(n_asg) - seg_start[eid_s]
    n_blk = -(-n_asg // MOE_BLOCK) + N_EXPERTS
    cap = n_blk * MOE_BLOCK
    slot_tok = jnp.zeros((cap,), jnp.int32).at[dest].set(tok_s)
    slot_w = jnp.zeros((cap,), jnp.float32).at[dest].set(gw_s)
    blk_exp = jnp.minimum(jnp.searchsorted(pad_end, jnp.arange(n_blk) * MOE_BLOCK, side='right'), N_EXPERTS - 1)

    def run(args):
        e, toks, w = args
        y = swiglu(t[toks], w_gate[e], w_up[e], w_down[e])
        return (y.astype(jnp.float32) * w[:, None]).astype(t.dtype)

    ys = lax.map(run, (blk_exp, slot_tok.reshape(n_blk, MOE_BLOCK), slot_w.reshape(n_blk, MOE_BLOCK)))
    out = jnp.zeros_like(t).at[slot_tok].add(ys.reshape(cap, d))
    return out.reshape(b, s, d)


def setup_inputs(seed: int = 0) -> dict:
    key = jax.random.key(seed)
    ks = jax.random.split(key, 32)

    def nrm(k, shape, fan_in):
        return jax.random.normal(k, shape, jnp.float32) * (fan_in ** -0.5)

    def gain(k, shape):
        return 1.0 + 0.02 * jax.random.normal(k, shape, jnp.float32)

    return {
        'x': jax.random.normal(ks[0], (BATCH, SEQ, D_MODEL), jnp.float32),
        'mem': jax.random.normal(ks[1], (BATCH, N_MEM, D_MODEL), jnp.float32),
        'attn_norm': gain(ks[2], (DEPTH, D_MODEL)),
        'w_in': nrm(ks[3], (DEPTH, D_MODEL, D_IN), D_MODEL),
        'conv_w': nrm(ks[4], (DEPTH, CONV_WIDTH, D_CONV), CONV_WIDTH),
        'q_norm': gain(ks[5], (DEPTH, HEAD_DIM)),
        'k_norm': gain(ks[6], (DEPTH, HEAD_DIM)),
        'mem_norm': gain(ks[7], (DEPTH, D_MODEL)),
        'w_mem_kv': nrm(ks[8], (DEPTH, D_MODEL, 2 * D_MEM), D_MODEL),
        'mq_norm': gain(ks[9], (DEPTH, MEM_HEAD_DIM)),
        'mk_norm': gain(ks[10], (DEPTH, MEM_HEAD_DIM)),
        'w_br_conv': nrm(ks[11], (DEPTH, D_CONV, D_MODEL), D_CONV),
        'w_br_attn': nrm(ks[12], (DEPTH, D_ATT, D_MODEL), D_ATT),
        'w_br_mem': nrm(ks[13], (DEPTH, D_MEM, D_MODEL), D_MEM),
        'w_out': nrm(ks[14], (DEPTH, D_MODEL, D_MODEL), D_MODEL),
        'ffn_norm': gain(ks[15], (DEPTH, D_MODEL)),
        'ff_gate': nrm(ks[16], (N_DENSE, D_MODEL, D_FF), D_MODEL),
        'ff_up': nrm(ks[17], (N_DENSE, D_MODEL, D_FF), D_MODEL),
        'ff_down': nrm(ks[18], (N_DENSE, D_FF, D_MODEL), D_FF),
        'router': nrm(ks[19], (N_MOE, D_MODEL, N_EXPERTS), D_MODEL),
        'ex_gate': nrm(ks[20], (N_MOE, N_EXPERTS, D_MODEL, D_FF), D_MODEL),
        'ex_up': nrm(ks[21], (N_MOE, N_EXPERTS, D_MODEL, D_FF), D_MODEL),
        'ex_down': nrm(ks[22], (N_MOE, N_EXPERTS, D_FF, D_MODEL), D_FF),
    }


def reference(x, mem, attn_norm, w_in, conv_w, q_norm, k_norm, mem_norm, w_mem_kv, mq_norm, mk_norm, w_br_conv, w_br_attn, w_br_mem, w_out, ffn_norm, ff_gate, ff_up, ff_down, router, ex_gate, ex_up, ex_down):
    s = x.shape[1]
    topk = min(MAX_TOPK, s // 4)
    rope_att = rope_tables(s, HEAD_DIM)
    rope_idx = rope_tables(s, IDX_DIM)
    for l in range(DEPTH):
        x = x + mixer_block(x, mem, attn_norm[l], w_in[l], conv_w[l], q_norm[l], k_norm[l], mem_norm[l], w_mem_kv[l], mq_norm[l], mk_norm[l], w_br_conv[l], w_br_attn[l], w_br_mem[l], w_out[l], rope_att, rope_idx, topk)
        h = rms_norm(x, ffn_norm[l])
        j = l // 2
        if l % 2 == 0:
            x = x + swiglu(h, ff_gate[j], ff_up[j], ff_down[j])
        else:
            x = x + moe_swiglu(h, router[j], ex_gate[j], ex_up[j], ex_down[j])
    return x
```

```python
import functools
import math

import numpy as np
import jax
import jax.numpy as jnp
from jax import lax
from jax.experimental import pallas as pl
from jax.experimental.pallas import tpu as pltpu

F32 = jnp.float32
BF16 = jnp.bfloat16
I32 = jnp.int32

N_HEADS = 8
HEAD_DIM = 128
N_IDX_HEADS = 8
IDX_DIM = 64
MAX_TOPK = 256
N_MEM_HEADS = 4
MEM_HEAD_DIM = 256
N_EXPERTS = 8
TOP_K = 2
CONV_WIDTH = 3
ROPE_THETA = 10000.0
EPS = 1e-6

LANES = 128
INT_MIN = -(2 ** 31)
LOG2E = math.log2(math.e)
VMEM_LIMIT = 56 * 1024 * 1024

COL_CB, COL_CC, COL_CU, COL_Q, COL_K, COL_V, COL_QM, COL_GC, COL_GA, COL_GM = range(10)
N_MAIN_BLOCKS = 10
IDX_COLS = 640


def _cparams(*sem):
    return pltpu.CompilerParams(dimension_semantics=sem, vmem_limit_bytes=VMEM_LIMIT)


def _rms(x, g):
    return x * lax.rsqrt(jnp.mean(x * x, axis=-1, keepdims=True) + EPS) * g


def _norm_matmul_kernel(x_ref, g_ref, w_ref, o_ref, h_ref):
    @pl.when(pl.program_id(1) == 0)
    def _():
        h_ref[...] = _rms(x_ref[...], g_ref[...]).astype(BF16)

    o_ref[...] = jnp.dot(h_ref[...], w_ref[...], preferred_element_type=F32)


def norm_matmul(x, g, w, tm, tn):
    m, k = x.shape
    n = w.shape[1]
    return pl.pallas_call(
        _norm_matmul_kernel,
        grid=(m // tm, n // tn),
        in_specs=[pl.BlockSpec((tm, k), lambda i, j: (i, 0)),
                  pl.BlockSpec((1, k), lambda i, j: (0, 0)),
                  pl.BlockSpec((k, tn), lambda i, j: (0, j))],
        out_specs=pl.BlockSpec((tm, tn), lambda i, j: (i, j)),
        out_shape=jax.ShapeDtypeStruct((m, n), F32),
        scratch_shapes=[pltpu.VMEM((tm, k), BF16)],
        compiler_params=_cparams("parallel", "arbitrary"),
        name="norm_matmul",
    )(x, g, w)


def _prep_kernel(q_ref, k_ref, v_ref, idx_ref, ca_ref, sa_ref, ci_ref, si_ref, qg_ref, kg_ref,
                 qn_ref, kn_ref, vb_ref, qip_ref, kil_ref, wis_ref, *, q_scale, w_scale):
    ca = ca_ref[...]
    sa = sa_ref[...]
    for h in range(N_HEADS):
        sl = slice(h * HEAD_DIM, (h + 1) * HEAD_DIM)
        for src, g_ref, dst, scale in ((q_ref, qg_ref, qn_ref, q_scale), (k_ref, kg_ref, kn_ref, 1.0)):
            n = _rms(src[:, sl], g_ref[...])
            o = n * ca + pltpu.roll(n, HEAD_DIM // 2, 1) * sa
            dst[:, sl] = (o * scale).astype(BF16)
    vb_ref[...] = v_ref[...].astype(BF16)

    ci = ci_ref[...]
    si = si_ref[...]
    lane = lax.broadcasted_iota(I32, ci.shape, 1)
    first_half = (lane & (IDX_DIM // 2)) == 0
    low_head = lane < IDX_DIM

    def idx_rope(x):
        rot = jnp.where(first_half, pltpu.roll(x, LANES - IDX_DIM // 2, 1), pltpu.roll(x, IDX_DIM // 2, 1))
        return x * ci + rot * si

    for c in range(N_IDX_HEADS // 2):
        o = idx_rope(idx_ref[:, c * LANES:(c + 1) * LANES])
        qip_ref[:, (2 * c) * LANES:(2 * c + 1) * LANES] = jnp.where(low_head, o, 0.0).astype(BF16)
        qip_ref[:, (2 * c + 1) * LANES:(2 * c + 2) * LANES] = jnp.where(
            low_head, pltpu.roll(o, IDX_DIM, 1), 0.0).astype(BF16)
    tail = idx_ref[:, N_IDX_HEADS * IDX_DIM:N_IDX_HEADS * IDX_DIM + LANES]
    kil_ref[...] = jnp.where(low_head, idx_rope(tail), 0.0).astype(BF16)
    wis_ref[...] = tail * w_scale


def dsa_prep(p_main, p_idx, rope, q_norm, k_norm, seq, ts):
    m = p_main.shape[0]
    d = N_HEADS * HEAD_DIM
    n_seq_tiles = seq // ts
    ca, sa, ci, si = rope
    row = lambda c: pl.BlockSpec((ts, d), lambda i, c=c: (i, c))
    tab = pl.BlockSpec((ts, LANES), lambda i: (i % n_seq_tiles, 0))
    gain = pl.BlockSpec((1, HEAD_DIM), lambda i: (0, 0))
    out_d = pl.BlockSpec((ts, d), lambda i: (i, 0))
    out_l = pl.BlockSpec((ts, LANES), lambda i: (i, 0))
    q_scale = HEAD_DIM ** -0.5 * LOG2E
    w_scale = (N_IDX_HEADS * IDX_DIM) ** -0.5
    return pl.pallas_call(
        functools.partial(_prep_kernel, q_scale=q_scale, w_scale=w_scale),
        grid=(m // ts,),
        in_specs=[row(COL_Q), row(COL_K), row(COL_V),
                  pl.BlockSpec((ts, IDX_COLS), lambda i: (i, 0)),
                  tab, tab, tab, tab, gain, gain],
        out_specs=[out_d, out_d, out_d, out_d, out_l, out_l],
        out_shape=[jax.ShapeDtypeStruct((m, d), BF16)] * 4
        + [jax.ShapeDtypeStruct((m, LANES), BF16), jax.ShapeDtypeStruct((m, LANES), F32)],
        compiler_params=_cparams("parallel"),
        name="dsa_prep",
    )(p_main, p_main, p_main, p_idx, ca, sa, ci, si, q_norm, k_norm)


SCORE_COLS = 512
ROW_GROUP = 64


def _dsa_kernel(qi_of, kb_of, last_of, nsc_of,
                qn_ref, qip_ref, wis_ref, kil_ref, kn_ref, vb_ref,
                o_ref,
                keys_ref, thr_ref, wb_ref, m_ref, l_ref, acc_ref,
                *, tq, tk, topk):
    p = pl.program_id(1)
    qi = qi_of[p]
    kb = kb_of[p]
    sub = SCORE_COLS // LANES
    kch = tk // LANES

    @pl.when(kb == 0)
    def _():
        w = wis_ref[...]
        for h in range(N_IDX_HEADS):
            wb_ref[h] = jnp.broadcast_to(w[:, IDX_DIM + h:IDX_DIM + h + 1], (tq, LANES))
        row_id = qi * tq + lax.broadcasted_iota(I32, (tq, SCORE_COLS), 0)
        col_iota = lax.broadcasted_iota(I32, (tq, SCORE_COLS), 1)

        def score_step(c, carry):
            start = pl.multiple_of(c * SCORE_COLS, SCORE_COLS)
            kblk = kil_ref[pl.ds(start, SCORE_COLS), :]
            sc = jnp.zeros((tq, SCORE_COLS), F32)
            for h in range(N_IDX_HEADS):
                z = lax.dot_general(qip_ref[:, h * LANES:(h + 1) * LANES], kblk,
                                    (((1,), (1,)), ((), ())), preferred_element_type=F32)
                sc = sc + jnp.maximum(z, 0.0) * jnp.tile(wb_ref[h], (1, sub))
            bits = pltpu.bitcast(sc, I32)
            key = jnp.where(bits < 0, bits ^ 0x7FFFFFFF, bits)
            key = jnp.where(col_iota + start <= row_id, key, INT_MIN)
            for j in range(sub):
                keys_ref[c * sub + j] = key[:, j * LANES:(j + 1) * LANES]
            return carry

        n_sc = nsc_of[p]
        lax.fori_loop(0, n_sc, score_step, 0)

        for rg in range(tq // ROW_GROUP):
            rows = slice(rg * ROW_GROUP, (rg + 1) * ROW_GROUP)

            def pass_body(i, prefix, rows=rows):
                cand = prefix + jnp.left_shift(jnp.int32(1), 31 - i)

                def count_step(c, acc):
                    for j in range(sub):
                        acc = acc + (keys_ref[c * sub + j, rows, :] >= cand).astype(I32)
                    return acc

                acc = lax.fori_loop(0, n_sc, count_step, jnp.zeros((ROW_GROUP, LANES), I32))
                cnt = jnp.sum(acc.astype(F32), axis=1, keepdims=True)
                return jnp.where(cnt >= topk, cand, prefix)

            prefix = lax.fori_loop(0, 32, pass_body, jnp.full((ROW_GROUP, LANES), INT_MIN, I32))
            thr_ref[rows, :] = jnp.maximum(prefix, INT_MIN + 1)

        m_ref[...] = jnp.full(m_ref.shape, -jnp.inf, F32)
        l_ref[...] = jnp.zeros(l_ref.shape, F32)
        acc_ref[...] = jnp.zeros(acc_ref.shape, F32)

    keys = jnp.concatenate([keys_ref[kb * kch + j] for j in range(kch)], axis=1)
    bias = jnp.where(keys >= jnp.tile(thr_ref[...], (1, kch)), 0.0, -jnp.inf).astype(F32)
    for h in range(N_HEADS):
        sl = slice(h * HEAD_DIM, (h + 1) * HEAD_DIM)
        s = lax.dot_general(qn_ref[:, sl], kn_ref[:, sl], (((1,), (1,)), ((), ())),
                            preferred_element_type=F32) + bias
        m_prev = m_ref[h]
        m_new = jnp.maximum(m_prev, jnp.max(s, axis=1, keepdims=True))
        m_safe = jnp.where(m_new == -jnp.inf, 0.0, m_new)
        alpha = jnp.exp2(m_prev - m_safe)
        pr = jnp.exp2(s - jnp.tile(m_safe, (1, kch)))
        l_ref[h] = alpha * l_ref[h] + jnp.sum(pr, axis=1, keepdims=True)
        m_ref[h] = m_new
        acc_ref[:, sl] = acc_ref[:, sl] * alpha + jnp.dot(
            pr.astype(BF16), vb_ref[:, sl], preferred_element_type=F32)

    @pl.when(last_of[p] == 1)
    def _():
        for h in range(N_HEADS):
            sl = slice(h * HEAD_DIM, (h + 1) * HEAD_DIM)
            o_ref[:, sl] = (acc_ref[:, sl] / l_ref[h]).astype(o_ref.dtype)


def _dsa_schedule(seq, tq, tk):
    qi_of, kb_of, last_of, nsc_of = [], [], [], []
    for qi in range(seq // tq):
        last_kb = ((qi + 1) * tq - 1) // tk
        for kb in range(last_kb + 1):
            qi_of.append(qi)
            kb_of.append(kb)
            last_of.append(int(kb == last_kb))
            nsc_of.append(-(-((qi + 1) * tq) // SCORE_COLS))
    return tuple(np.asarray(a, np.int32) for a in (qi_of, kb_of, last_of, nsc_of))


def dsa_attention(qn, qip, wis, kil, kn, vb, batch, seq, topk, tq=256, tk=512):
    d = N_HEADS * HEAD_DIM
    nq = seq // tq
    nk = seq // tk
    sched = _dsa_schedule(seq, tq, tk)
    n_steps = len(sched[0])
    qrow = lambda w: pl.BlockSpec((tq, w), lambda b, p, qi_of, kb_of, lo, ns: (b * nq + qi_of[p], 0))
    krow = pl.BlockSpec((tk, d), lambda b, p, qi_of, kb_of, lo, ns: (b * nk + kb_of[p], 0))
    grid_spec = pltpu.PrefetchScalarGridSpec(
        num_scalar_prefetch=4,
        grid=(batch, n_steps),
        in_specs=[qrow(d), qrow(d), qrow(LANES),
                  pl.BlockSpec((seq, LANES), lambda b, p, *_: (b, 0)),
                  krow, krow],
        out_specs=qrow(d),
        scratch_shapes=[pltpu.VMEM((seq // LANES, tq, LANES), I32),
                        pltpu.VMEM((tq, LANES), I32),
                        pltpu.VMEM((N_IDX_HEADS, tq, LANES), F32),
                        pltpu.VMEM((N_HEADS, tq, LANES), F32),
                        pltpu.VMEM((N_HEADS, tq, LANES), F32),
                        pltpu.VMEM((tq, d), F32)],
    )
    return pl.pallas_call(
        functools.partial(_dsa_kernel, tq=tq, tk=tk, topk=topk),
        grid_spec=grid_spec,
        out_shape=jax.ShapeDtypeStruct((batch * seq, d), BF16),
        compiler_params=_cparams("parallel", "arbitrary"),
        name="dsa_attention",
    )(*sched, qn, qip, wis, kil, kn, vb)


def _mem_kv_kernel(mem_ref, g_ref, w_ref, kg_ref, mk_ref, mv_ref):
    d = N_MEM_HEADS * MEM_HEAD_DIM
    h = _rms(mem_ref[...], g_ref[...]).astype(BF16)
    kv = jnp.dot(h, w_ref[...], preferred_element_type=F32)
    for hd in range(N_MEM_HEADS):
        sl = slice(hd * MEM_HEAD_DIM, (hd + 1) * MEM_HEAD_DIM)
        mk_ref[:, sl] = _rms(kv[:, sl], kg_ref[...]).astype(BF16)
    mv_ref[...] = kv[:, d:].astype(BF16)


def mem_kv(mem2d, mem_norm, w_mem_kv, mk_norm, batch, n_mem):
    dm = mem2d.shape[1]
    d = N_MEM_HEADS * MEM_HEAD_DIM
    blk = pl.BlockSpec((n_mem, d), lambda b: (b, 0))
    return pl.pallas_call(
        _mem_kv_kernel,
        grid=(batch,),
        in_specs=[pl.BlockSpec((n_mem, dm), lambda b: (b, 0)),
                  pl.BlockSpec((1, dm), lambda b: (0, 0)),
                  pl.BlockSpec((dm, 2 * d), lambda b: (0, 0)),
                  pl.BlockSpec((1, MEM_HEAD_DIM), lambda b: (0, 0))],
        out_specs=[blk, blk],
        out_shape=[jax.ShapeDtypeStruct((batch * n_mem, d), BF16)] * 2,
        compiler_params=_cparams("arbitrary"),
        name="mem_kv",
    )(mem2d, mem_norm, w_mem_kv, mk_norm)


def _mem_attn_kernel(q_ref, mk_ref, mv_ref, g_ref, o_ref, *, scale):
    for hd in range(N_MEM_HEADS):
        sl = slice(hd * MEM_HEAD_DIM, (hd + 1) * MEM_HEAD_DIM)
        qh = (_rms(q_ref[:, sl], g_ref[...]) * scale).astype(BF16)
        s = lax.dot_general(qh, mk_ref[:, sl], (((1,), (1,)), ((), ())), preferred_element_type=F32)
        pr = jnp.exp2(s - jnp.max(s, axis=1, keepdims=True))
        o = jnp.dot(pr.astype(BF16), mv_ref[:, sl], preferred_element_type=F32)
        o_ref[:, sl] = (o / jnp.sum(pr, axis=1, keepdims=True)).astype(o_ref.dtype)


def mem_attention(p_main, mk, mv, mq_norm, seq, n_mem, tm):
    m = p_main.shape[0]
    d = N_MEM_HEADS * MEM_HEAD_DIM
    tiles_per_seq = seq // tm
    kv = pl.BlockSpec((n_mem, d), lambda i: (i // tiles_per_seq, 0))
    return pl.pallas_call(
        functools.partial(_mem_attn_kernel, scale=MEM_HEAD_DIM ** -0.5 * LOG2E),
        grid=(m // tm,),
        in_specs=[pl.BlockSpec((tm, d), lambda i: (i, COL_QM)), kv, kv,
                  pl.BlockSpec((1, MEM_HEAD_DIM), lambda i: (0, 0))],
        out_specs=pl.BlockSpec((tm, d), lambda i: (i, 0)),
        out_shape=jax.ShapeDtypeStruct((m, d), BF16),
        compiler_params=_cparams("parallel"),
        name="mem_attention",
    )(p_main, mk, mv, mq_norm)


HALO = 8


def _merge_kernel(x_ref, cb_ref, cc_ref, cu_ref, cch_ref, cuh_ref, gc_ref, ga_ref, gm_ref,
                  att_ref, mem_ref, cw_ref, wc_ref, wa_ref, wm_ref, wo_ref, o_ref, *, tiles_per_seq):
    i = pl.program_id(0)
    pcur = cc_ref[...] * cu_ref[...]
    halo = jnp.where(i % tiles_per_seq == 0, 0.0, cch_ref[...] * cuh_ref[...])
    row = lax.broadcasted_iota(I32, pcur.shape, 0)
    h1 = halo[HALO - 1:HALO, :]
    h2 = halo[HALO - 2:HALO - 1, :]
    p1 = jnp.where(row == 0, h1, pltpu.roll(pcur, 1, 0))
    p2 = jnp.where(row == 0, h2, jnp.where(row == 1, h1, pltpu.roll(pcur, 2, 0)))
    cw = cw_ref[...]
    conv = cb_ref[...] * (cw[0:1, :] * p2 + cw[1:2, :] * p1 + cw[2:3, :] * pcur)
    merged = jax.nn.sigmoid(gc_ref[...]) * jnp.dot(conv.astype(BF16), wc_ref[...], preferred_element_type=F32)
    merged += jax.nn.sigmoid(ga_ref[...]) * jnp.dot(att_ref[...], wa_ref[...], preferred_element_type=F32)
    merged += jax.nn.sigmoid(gm_ref[...]) * jnp.dot(mem_ref[...], wm_ref[...], preferred_element_type=F32)
    o_ref[...] = x_ref[...] + jnp.dot(merged.astype(BF16), wo_ref[...], preferred_element_type=F32)


def merge(x2d, p_main, att, memo, conv_w, w_c, w_a, w_m, w_o, seq, tm):
    m, d = x2d.shape
    tiles_per_seq = seq // tm
    col = lambda c: pl.BlockSpec((tm, d), lambda i, c=c: (i, c))
    halo = lambda c: pl.BlockSpec((HALO, d), lambda i, c=c: (jnp.maximum(i * (tm // HALO) - 1, 0), c))
    rowb = pl.BlockSpec((tm, d), lambda i: (i, 0))
    wspec = pl.BlockSpec((d, d), lambda i: (0, 0))
    return pl.pallas_call(
        functools.partial(_merge_kernel, tiles_per_seq=tiles_per_seq),
        grid=(m // tm,),
        in_specs=[rowb, col(COL_CB), col(COL_CC), col(COL_CU), halo(COL_CC), halo(COL_CU),
                  col(COL_GC), col(COL_GA), col(COL_GM), rowb, rowb,
                  pl.BlockSpec((CONV_WIDTH, d), lambda i: (0, 0)),
                  wspec, wspec, wspec, wspec],
        out_specs=rowb,
        out_shape=jax.ShapeDtypeStruct((m, d), F32),
        compiler_params=_cparams("parallel"),
        name="merge",
    )(x2d, p_main, p_main, p_main, p_main, p_main, p_main, p_main, p_main, att, memo,
      conv_w, w_c, w_a, w_m, w_o)


def _ffn_kernel(x_ref, g_ref, wg_ref, wu_ref, wd_ref, o_ref, h_ref, acc_ref):
    j = pl.program_id(1)

    @pl.when(j == 0)
    def _():
        h_ref[...] = _rms(x_ref[...], g_ref[...]).astype(BF16)
        acc_ref[...] = jnp.zeros(acc_ref.shape, F32)

    h = h_ref[...]
    a = jax.nn.silu(jnp.dot(h, wg_ref[...], preferred_element_type=F32)) * jnp.dot(
        h, wu_ref[...], preferred_element_type=F32)
    acc_ref[...] += jnp.dot(a.astype(BF16), wd_ref[...], preferred_element_type=F32)

    @pl.when(j == pl.num_programs(1) - 1)
    def _():
        o_ref[...] = x_ref[...] + acc_ref[...]


def ffn_dense(x2d, g, w_gate, w_up, w_down, tm, tf):
    m, d = x2d.shape
    ff = w_gate.shape[1]
    return pl.pallas_call(
        _ffn_kernel,
        grid=(m // tm, ff // tf),
        in_specs=[pl.BlockSpec((tm, d), lambda i, j: (i, 0)),
                  pl.BlockSpec((1, d), lambda i, j: (0, 0)),
                  pl.BlockSpec((d, tf), lambda i, j: (0, j)),
                  pl.BlockSpec((d, tf), lambda i, j: (0, j)),
                  pl.BlockSpec((tf, d), lambda i, j: (j, 0))],
        out_specs=pl.BlockSpec((tm, d), lambda i, j: (i, 0)),
        out_shape=jax.ShapeDtypeStruct((m, d), F32),
        scratch_shapes=[pltpu.VMEM((tm, d), BF16), pltpu.VMEM((tm, d), F32)],
        compiler_params=_cparams("parallel", "arbitrary"),
        name="ffn_dense",
    )(x2d, g, w_gate, w_up, w_down)


def _router_kernel(x_ref, g_ref, r_ref, h_ref, eid_ref, gw_ref):
    h = _rms(x_ref[...], g_ref[...])
    h_ref[...] = h
    logits = jnp.dot(h, r_ref[...], preferred_element_type=F32, precision=lax.Precision.HIGHEST)
    lane = lax.broadcasted_iota(I32, logits.shape, 1)
    lanef = lane.astype(F32)
    logits = jnp.where(lane < N_EXPERTS, logits, -jnp.inf)
    m1 = jnp.max(logits, axis=1, keepdims=True)
    i1 = jnp.min(jnp.where(logits == m1, lanef, float(LANES)), axis=1, keepdims=True)
    rest = jnp.where(lanef == i1, -jnp.inf, logits)
    m2 = jnp.max(rest, axis=1, keepdims=True)
    i2 = jnp.min(jnp.where(rest == m2, lanef, float(LANES)), axis=1, keepdims=True)
    e = jnp.exp(m2 - m1)
    g1 = 1.0 / (1.0 + e)
    eid_ref[...] = jnp.where(lane == 0, i1, jnp.where(lane == 1, i2, 0.0)).astype(I32)
    gw_ref[...] = jnp.where(lane == 0, g1, jnp.where(lane == 1, e * g1, 0.0))


def router(x2d, g, router_pad, tm):
    m, d = x2d.shape
    lane_out = pl.BlockSpec((tm, LANES), lambda i: (i, 0))
    return pl.pallas_call(
        _router_kernel,
        grid=(m // tm,),
        in_specs=[pl.BlockSpec((tm, d), lambda i: (i, 0)),
                  pl.BlockSpec((1, d), lambda i: (0, 0)),
                  pl.BlockSpec((d, LANES), lambda i: (0, 0))],
        out_specs=[pl.BlockSpec((tm, d), lambda i: (i, 0)), lane_out, lane_out],
        out_shape=[jax.ShapeDtypeStruct((m, d), F32),
                   jax.ShapeDtypeStruct((m, LANES), I32),
                   jax.ShapeDtypeStruct((m, LANES), F32)],
        compiler_params=_cparams("parallel"),
        name="router",
    )(x2d, g, router_pad)


def _expert_kernel(blk_exp, n_used, tok_ref, w_ref, h_ref, wg_ref, wu_ref, wd_ref, y_ref,
                   xbuf, xb_ref, acc_ref, sem, *, mb):
    b = pl.program_id(0)
    j = pl.program_id(1)
    active = b < n_used[0]
    last = j == pl.num_programs(1) - 1

    def row_copy(r):
        return pltpu.make_async_copy(h_ref.at[pl.ds(tok_ref[0, 0, r], 1)], xbuf.at[pl.ds(r, 1)], sem.at[0])

    @pl.when(jnp.logical_and(active, j == 0))
    def _():
        def start(r, c):
            row_copy(r).start()
            return c

        lax.fori_loop(0, mb, start, 0)

        def wait(r, c):
            row_copy(r).wait()
            return c

        lax.fori_loop(0, mb, wait, 0)
        xb_ref[...] = xbuf[...].astype(BF16)
        acc_ref[...] = jnp.zeros(acc_ref.shape, F32)

    @pl.when(active)
    def _():
        x = xb_ref[...]
        a = jax.nn.silu(jnp.dot(x, wg_ref[0], preferred_element_type=F32)) * jnp.dot(
            x, wu_ref[0], preferred_element_type=F32)
        acc_ref[...] += jnp.dot(a.astype(BF16), wd_ref[0], preferred_element_type=F32)

    @pl.when(jnp.logical_and(active, last))
    def _():
        y_ref[...] = acc_ref[...] * w_ref[...]

    @pl.when(jnp.logical_and(jnp.logical_not(active), last))
    def _():
        y_ref[...] = jnp.zeros(y_ref.shape, F32)


def expert_ffn(hb, slot_tok, slot_w, blk_exp, n_used, ex_gate, ex_up, ex_down, mb, tf):
    m, d = hb.shape
    n_blk = blk_exp.shape[0]
    ff = ex_gate.shape[2]
    grid_spec = pltpu.PrefetchScalarGridSpec(
        num_scalar_prefetch=2,
        grid=(n_blk, ff // tf),
        in_specs=[pl.BlockSpec((1, 1, mb), lambda b, j, be, nu: (b, 0, 0), memory_space=pltpu.SMEM),
                  pl.BlockSpec((mb, 1), lambda b, j, be, nu: (b, 0)),
                  pl.BlockSpec(memory_space=pl.ANY),
                  pl.BlockSpec((1, d, tf), lambda b, j, be, nu: (be[b], 0, j)),
                  pl.BlockSpec((1, d, tf), lambda b, j, be, nu: (be[b], 0, j)),
                  pl.BlockSpec((1, tf, d), lambda b, j, be, nu: (be[b], j, 0))],
        out_specs=pl.BlockSpec((mb, d), lambda b, j, be, nu: (b, 0)),
        scratch_shapes=[pltpu.VMEM((mb, d), F32), pltpu.VMEM((mb, d), BF16), pltpu.VMEM((mb, d), F32),
                        pltpu.SemaphoreType.DMA((1,))],
    )
    return pl.pallas_call(
        functools.partial(_expert_kernel, mb=mb),
        grid_spec=grid_spec,
        out_shape=jax.ShapeDtypeStruct((n_blk * mb, d), F32),
        compiler_params=_cparams("arbitrary", "arbitrary"),
        name="expert_ffn",
    )(blk_exp, n_used, slot_tok.reshape(n_blk, 1, mb), slot_w.reshape(n_blk * mb, 1), hb,
      ex_gate, ex_up, ex_down)


def _combine_kernel(pos_ref, x_ref, ys_ref, o_ref, buf, sem, *, tm):
    def row_copy(r):
        return pltpu.make_async_copy(ys_ref.at[pl.ds(pos_ref[0, 0, r], 1)], buf.at[pl.ds(r, 1)], sem.at[0])

    def start(r, c):
        row_copy(r).start()
        return c

    lax.fori_loop(0, TOP_K * tm, start, 0)

    def wait(r, c):
        row_copy(r).wait()
        return c

    lax.fori_loop(0, TOP_K * tm, wait, 0)
    out = x_ref[...]
    for kk in range(TOP_K):
        out = out + buf[kk * tm:(kk + 1) * tm, :]
    o_ref[...] = out


def moe_combine(x2d, ys, pos, tm):
    m, d = x2d.shape
    n_tiles = m // tm
    return pl.pallas_call(
        functools.partial(_combine_kernel, tm=tm),
        grid=(n_tiles,),
        in_specs=[pl.BlockSpec((1, 1, TOP_K * tm), lambda i: (i, 0, 0), memory_space=pltpu.SMEM),
                  pl.BlockSpec((tm, d), lambda i: (i, 0)),
                  pl.BlockSpec(memory_space=pl.ANY)],
        out_specs=pl.BlockSpec((tm, d), lambda i: (i, 0)),
        out_shape=jax.ShapeDtypeStruct((m, d), F32),
        scratch_shapes=[pltpu.VMEM((TOP_K * tm, d), F32), pltpu.SemaphoreType.DMA((1,))],
        compiler_params=_cparams("arbitrary"),
        name="moe_combine",
    )(pos, x2d, ys)


def _moe_plan(eid, gw, n_tok, mb):
    n_asg = n_tok * TOP_K
    n_blk = n_asg // mb + N_EXPERTS
    cap = n_blk * mb
    e = eid.reshape(-1)
    tok = jnp.repeat(jnp.arange(n_tok, dtype=I32), TOP_K)
    onehot = (e[:, None] == jnp.arange(N_EXPERTS, dtype=I32)[None, :]).astype(I32)
    rank = jnp.cumsum(onehot, axis=0) - onehot
    rank = jnp.sum(rank * onehot, axis=1)
    counts = jnp.sum(onehot, axis=0)
    padded = (counts + mb - 1) // mb * mb
    pad_end = jnp.cumsum(padded)
    pad_start = pad_end - padded
    dest = (pad_start[e] + rank).astype(I32)
    slot_tok = jnp.zeros((cap,), I32).at[dest].set(tok)
    slot_w = jnp.zeros((cap,), F32).at[dest].set(gw.reshape(-1))
    blk_start = jnp.arange(n_blk, dtype=I32) * mb
    blk_exp = jnp.minimum(jnp.searchsorted(pad_end, blk_start, side='right'), N_EXPERTS - 1).astype(I32)
    n_used = (pad_end[-1] // mb).astype(I32).reshape(1)
    return slot_tok, slot_w, blk_exp, n_used, dest.reshape(n_tok, TOP_K)


def moe_layer(x2d, g, router_w, ex_gate, ex_up, ex_down, mb=512, tf=512, tm_route=512, tm_comb=256):
    n_tok, d = x2d.shape
    router_pad = jnp.zeros((d, LANES), F32).at[:, :N_EXPERTS].set(router_w)
    hb, eid, gw = router(x2d, g, router_pad, tm_route)
    slot_tok, slot_w, blk_exp, n_used, dest = _moe_plan(eid[:, :TOP_K], gw[:, :TOP_K], n_tok, mb)
    ys = expert_ffn(hb, slot_tok, slot_w, blk_exp, n_used, ex_gate, ex_up, ex_down, mb, tf)
    pos = dest.reshape(n_tok // tm_comb, tm_comb, TOP_K).transpose(0, 2, 1).reshape(
        n_tok // tm_comb, 1, TOP_K * tm_comb)
    return moe_combine(x2d, ys, pos, tm_comb)


def _rope_tables(seq):
    def tab(dim):
        inv = 1.0 / (ROPE_THETA ** (jnp.arange(0, dim, 2, dtype=F32) / dim))
        ang = jnp.arange(seq, dtype=F32)[:, None] * inv[None, :]
        cos, sin = jnp.cos(ang), jnp.sin(ang)
        reps = LANES // dim
        return (jnp.tile(jnp.concatenate([cos, cos], axis=1), (1, reps)),
                jnp.tile(jnp.concatenate([-sin, sin], axis=1), (1, reps)))

    ca, sa = tab(HEAD_DIM)
    ci, si = tab(IDX_DIM)
    return ca, sa, ci, si


def _pack_w_in(w):
    d_conv = d_att = N_HEADS * HEAD_DIM
    n_first = 3 * d_conv + 3 * d_att
    n_idx = N_IDX_HEADS * IDX_DIM + IDX_DIM + N_IDX_HEADS
    main = jnp.concatenate([w[:, :n_first], w[:, n_first + n_idx:]], axis=1)
    idx = jnp.pad(w[:, n_first:n_first + n_idx], ((0, 0), (0, IDX_COLS - n_idx)))
    return main.astype(BF16), idx.astype(BF16)


def kernel(x, mem, attn_norm, w_in, conv_w, q_norm, k_norm, mem_norm, w_mem_kv, mq_norm, mk_norm, w_br_conv, w_br_attn, w_br_mem, w_out, ffn_norm, ff_gate, ff_up, ff_down, router, ex_gate, ex_up, ex_down):
    batch, seq, d = x.shape
    depth = attn_norm.shape[0]
    n_mem = mem.shape[1]
    topk = min(MAX_TOPK, seq // 4)
    m = batch * seq
    rope = _rope_tables(seq)
    mem2d = mem.reshape(batch * n_mem, d)
    xc = x.reshape(m, d)
    row = lambda a: a.reshape(1, -1)
    for l in range(depth):
        w_main, w_idx = _pack_w_in(w_in[l])
        g = row(attn_norm[l])
        p_main = norm_matmul(xc, g, w_main, tm=1024, tn=1024)
        p_idx = norm_matmul(xc, g, w_idx, tm=1024, tn=IDX_COLS)
        qn, kn, vb, qip, kil, wis = dsa_prep(p_main, p_idx, rope, row(q_norm[l]), row(k_norm[l]), seq, ts=512)
        att = dsa_attention(qn, qip, wis, kil, kn, vb, batch, seq, topk)
        mk, mv = mem_kv(mem2d, row(mem_norm[l]), w_mem_kv[l].astype(BF16), row(mk_norm[l]), batch, n_mem)
        memo = mem_attention(p_main, mk, mv, row(mq_norm[l]), seq, n_mem, tm=512)
        xc = merge(xc, p_main, att, memo, conv_w[l], w_br_conv[l].astype(BF16), w_br_attn[l].astype(BF16),
                   w_br_mem[l].astype(BF16), w_out[l].astype(BF16), seq, tm=256)
        j = l // 2
        if l % 2 == 0:
            xc = ffn_dense(xc, row(ffn_norm[l]), ff_gate[j].astype(BF16), ff_up[j].astype(BF16),
                           ff_down[j].astype(BF16), tm=512, tf=512)
        else:
            xc = moe_layer(xc, row(ffn_norm[l]), router[j], ex_gate[j].astype(BF16), ex_up[j].astype(BF16),
                           ex_down[j].astype(BF16))
    return xc.reshape(batch, seq, d)
```

```python
import functools
import math

import numpy as np
import jax
import jax.numpy as jnp
from jax import lax
from jax.experimental import pallas as pl
from jax.experimental.pallas import tpu as pltpu

F32 = jnp.float32
BF16 = jnp.bfloat16
I32 = jnp.int32

N_HEADS = 8
HEAD_DIM = 128
N_IDX_HEADS = 8
IDX_DIM = 64
MAX_TOPK = 256
N_MEM_HEADS = 4
MEM_HEAD_DIM = 256
N_EXPERTS = 8
TOP_K = 2
CONV_WIDTH = 3
ROPE_THETA = 10000.0
EPS = 1e-6

LANES = 128
INT_MIN = -(2 ** 31)
INT_MAX = 2 ** 31 - 1
LOG2E = math.log2(math.e)
VMEM_LIMIT = 56 * 1024 * 1024

COL_CB, COL_CC, COL_CU, COL_Q, COL_K, COL_QM, COL_GC, COL_GA, COL_GM = range(9)
IDX_COLS = 640


def _cparams(*sem):
    return pltpu.CompilerParams(dimension_semantics=sem, vmem_limit_bytes=VMEM_LIMIT)


def _rms(x, g):
    return x * lax.rsqrt(jnp.mean(x * x, axis=-1, keepdims=True) + EPS) * g


def _norm_matmul_kernel(x_ref, g_ref, w_ref, o_ref, h_ref):
    @pl.when(pl.program_id(1) == 0)
    def _():
        h_ref[...] = _rms(x_ref[...], g_ref[...]).astype(BF16)

    o_ref[...] = jnp.dot(h_ref[...], w_ref[...], preferred_element_type=F32)


def norm_matmul(x, g, w, tm, tn):
    m, k = x.shape
    n = w.shape[1]
    return pl.pallas_call(
        _norm_matmul_kernel,
        grid=(m // tm, n // tn),
        in_specs=[pl.BlockSpec((tm, k), lambda i, j: (i, 0)),
                  pl.BlockSpec((1, k), lambda i, j: (0, 0)),
                  pl.BlockSpec((k, tn), lambda i, j: (0, j))],
        out_specs=pl.BlockSpec((tm, tn), lambda i, j: (i, j)),
        out_shape=jax.ShapeDtypeStruct((m, n), F32),
        scratch_shapes=[pltpu.VMEM((tm, k), BF16)],
        compiler_params=_cparams("parallel", "arbitrary"),
        name="norm_matmul",
    )(x, g, w)


def _norm_matmul_t_kernel(x_ref, g_ref, wt_ref, o_ref):
    h = _rms(x_ref[...], g_ref[...]).astype(BF16)
    o_ref[...] = lax.dot_general(wt_ref[...], h, (((1,), (1,)), ((), ())),
                                 preferred_element_type=F32).astype(o_ref.dtype)


def norm_matmul_t(x, g, wt, seq, tm):
    m, k = x.shape
    n = wt.shape[0]
    tiles_per_seq = seq // tm
    return pl.pallas_call(
        _norm_matmul_t_kernel,
        grid=(m // tm,),
        in_specs=[pl.BlockSpec((tm, k), lambda i: (i, 0)),
                  pl.BlockSpec((1, k), lambda i: (0, 0)),
                  pl.BlockSpec((n, k), lambda i: (0, 0))],
        out_specs=pl.BlockSpec((n, tm), lambda i: (i // tiles_per_seq, i % tiles_per_seq)),
        out_shape=jax.ShapeDtypeStruct((m // seq * n, seq), BF16),
        compiler_params=_cparams("parallel"),
        name="norm_matmul_t",
    )(x, g, wt)


def _prep_kernel(q_ref, k_ref, idx_ref, ca_ref, sa_ref, ci_ref, si_ref, qg_ref, kg_ref,
                 qn_ref, kn_ref, qip_ref, kil_ref, wis_ref, *, q_scale, w_scale):
    ca = ca_ref[...]
    sa = sa_ref[...]
    for h in range(N_HEADS):
        sl = slice(h * HEAD_DIM, (h + 1) * HEAD_DIM)
        for src, g_ref, dst, scale in ((q_ref, qg_ref, qn_ref, q_scale), (k_ref, kg_ref, kn_ref, 1.0)):
            n = _rms(src[:, sl], g_ref[...])
            o = n * ca + pltpu.roll(n, HEAD_DIM // 2, 1) * sa
            dst[:, sl] = (o * scale).astype(BF16)

    ci = ci_ref[...]
    si = si_ref[...]
    lane = lax.broadcasted_iota(I32, ci.shape, 1)
    first_half = (lane & (IDX_DIM // 2)) == 0
    low_head = lane < IDX_DIM

    def idx_rope(x):
        rot = jnp.where(first_half, pltpu.roll(x, LANES - IDX_DIM // 2, 1), pltpu.roll(x, IDX_DIM // 2, 1))
        return x * ci + rot * si

    for c in range(N_IDX_HEADS // 2):
        o = idx_rope(idx_ref[:, c * LANES:(c + 1) * LANES])
        qip_ref[:, (2 * c) * LANES:(2 * c + 1) * LANES] = jnp.where(low_head, o, 0.0).astype(BF16)
        qip_ref[:, (2 * c + 1) * LANES:(2 * c + 2) * LANES] = jnp.where(
            low_head, pltpu.roll(o, IDX_DIM, 1), 0.0).astype(BF16)
    tail = idx_ref[:, N_IDX_HEADS * IDX_DIM:N_IDX_HEADS * IDX_DIM + LANES]
    kil_ref[...] = jnp.where(low_head, idx_rope(tail), 0.0).astype(BF16)
    wis_ref[...] = tail * w_scale


def dsa_prep(p_main, p_idx, rope, q_norm, k_norm, seq, ts):
    m = p_main.shape[0]
    d = N_HEADS * HEAD_DIM
    n_seq_tiles = seq // ts
    ca, sa, ci, si = rope
    row = lambda c: pl.BlockSpec((ts, d), lambda i, c=c: (i, c))
    tab = pl.BlockSpec((ts, LANES), lambda i: (i % n_seq_tiles, 0))
    gain = pl.BlockSpec((1, HEAD_DIM), lambda i: (0, 0))
    out_d = pl.BlockSpec((ts, d), lambda i: (i, 0))
    out_l = pl.BlockSpec((ts, LANES), lambda i: (i, 0))
    q_scale = HEAD_DIM ** -0.5 * LOG2E
    w_scale = (N_IDX_HEADS * IDX_DIM) ** -0.5
    return pl.pallas_call(
        functools.partial(_prep_kernel, q_scale=q_scale, w_scale=w_scale),
        grid=(m // ts,),
        in_specs=[row(COL_Q), row(COL_K),
                  pl.BlockSpec((ts, IDX_COLS), lambda i: (i, 0)),
                  tab, tab, tab, tab, gain, gain],
        out_specs=[out_d, out_d, out_d, out_l, out_l],
        out_shape=[jax.ShapeDtypeStruct((m, d), BF16)] * 3
        + [jax.ShapeDtypeStruct((m, LANES), BF16), jax.ShapeDtypeStruct((m, LANES), F32)],
        compiler_params=_cparams("parallel"),
        name="dsa_prep",
    )(p_main, p_main, p_idx, ca, sa, ci, si, q_norm, k_norm)


SCORE_ROWS = 512
SLAB = 32
NT_DIMS = (((1,), (1,)), ((), ()))


def _dsa_kernel(qi_of, kb_of, last_of, nsc_of, bounded,
                qn_ref, qip_ref, wis_ref, kil_ref, kn_ref, vt_ref,
                o_ref,
                keys_ref, thr_ref, w_ref, m_ref, l_ref, acc_ref, s_ref, gmax_ref,
                *, tq, tk, topk):
    p = pl.program_id(1)
    qi = qi_of[p]
    kb = kb_of[p]

    @pl.when(kb == 0)
    def _():
        w_ref[...] = wis_ref[...].T[IDX_DIM:IDX_DIM + N_IDX_HEADS, :]
        q_id = qi * tq + lax.broadcasted_iota(I32, (SCORE_ROWS, tq), 1)
        k_iota = lax.broadcasted_iota(I32, (SCORE_ROWS, tq), 0)
        n_sc = nsc_of[p]

        def score_step(c, carry):
            start = pl.multiple_of(c * SCORE_ROWS, SCORE_ROWS)
            kblk = kil_ref[pl.ds(start, SCORE_ROWS), :]
            sc = jnp.zeros((SCORE_ROWS, tq), F32)
            for h in range(N_IDX_HEADS):
                z = lax.dot_general(kblk, qip_ref[:, h * LANES:(h + 1) * LANES], NT_DIMS,
                                    preferred_element_type=F32)
                sc = sc + jnp.maximum(z, 0.0) * w_ref[h:h + 1, :]
            bits = pltpu.bitcast(sc, I32)
            key = jnp.where(bits < 0, bits ^ 0x7FFFFFFF, bits)
            key = jnp.where(k_iota + start <= q_id, key, INT_MIN)
            keys_ref[pl.ds(start, SCORE_ROWS), :] = key
            gmax = gmax_ref[...]
            for j in range(SCORE_ROWS // MAX_TOPK):
                gmax = jnp.maximum(gmax, key[j * MAX_TOPK:(j + 1) * MAX_TOPK, :])
            gmax_ref[...] = gmax
            return carry

        gmax_ref[...] = jnp.full(gmax_ref.shape, INT_MIN, I32)
        lax.fori_loop(0, n_sc, score_step, 0)

        def count_ge(cand):
            def count_step(c, acc):
                start = pl.multiple_of(c * SCORE_ROWS, SCORE_ROWS)
                for j in range(SCORE_ROWS // SLAB):
                    acc = acc + (keys_ref[pl.ds(start + j * SLAB, SLAB), :] >= cand).astype(I32)
                return acc

            acc = lax.fori_loop(0, n_sc, count_step, jnp.zeros((SLAB, tq), I32))
            return jnp.sum(acc.astype(F32), axis=0, keepdims=True)

        gmax = gmax_ref[...]
        lo0 = jnp.maximum(jnp.min(gmax, axis=0, keepdims=True), INT_MIN + 1)
        hi0 = jnp.minimum(jnp.max(gmax, axis=0, keepdims=True), INT_MAX - 1) + 1

        def open_interval(lo, hi):
            return jnp.max(((lo + 1) < hi).astype(I32)) > 0

        def bisect(state):
            lo, hi, _ = state
            mid = (lo & hi) + ((lo ^ hi) >> 1)
            cnt = count_ge(mid)
            keep = cnt >= topk
            lo = jnp.where(keep, mid, lo)
            hi = jnp.where(cnt == topk, mid + 1, jnp.where(keep, hi, mid))
            return lo, hi, open_interval(lo, hi)

        lo, _, _ = lax.while_loop(lambda st: st[2], bisect, (lo0, hi0, open_interval(lo0, hi0)))
        thr_ref[...] = lo

        m_ref[...] = jnp.full(m_ref.shape, -jnp.inf, F32)
        l_ref[...] = jnp.zeros(l_ref.shape, F32)
        acc_ref[...] = jnp.zeros(acc_ref.shape, F32)

    def mask_bias():
        keys = keys_ref[pl.ds(pl.multiple_of(kb * tk, tk), tk), :]
        return jnp.where(keys >= thr_ref[...], 0.0, -jnp.inf).astype(F32)

    def masked_logits(h, bias):
        sl = slice(h * HEAD_DIM, (h + 1) * HEAD_DIM)
        return lax.dot_general(kn_ref[:, sl], qn_ref[:, sl], NT_DIMS, preferred_element_type=F32) + bias


    @pl.when(bounded[0] == 1)
    def _():
        bias = mask_bias()
        s_ref[0] = masked_logits(0, bias)
        for h in range(N_HEADS):
            sl = slice(h * HEAD_DIM, (h + 1) * HEAD_DIM)
            if h + 1 < N_HEADS:
                s_ref[(h + 1) % 2] = masked_logits(h + 1, bias)
            pr = jnp.exp2(s_ref[h % 2])
            l_ref[h:h + 1, :] = l_ref[h:h + 1, :] + jnp.sum(pr, axis=0, keepdims=True)
            acc_ref[sl, :] = acc_ref[sl, :] + jnp.dot(
                vt_ref[sl, :], pr.astype(BF16), preferred_element_type=F32)

    @pl.when(bounded[0] == 0)
    def _():
        bias = mask_bias()

        def logits(h):
            s = masked_logits(h, bias)
            s_ref[h % 2] = s
            return jnp.max(s, axis=0, keepdims=True)

        def accumulate(h, m_cur):
            sl = slice(h * HEAD_DIM, (h + 1) * HEAD_DIM)
            m_prev = m_ref[h:h + 1, :]
            m_new = jnp.maximum(m_prev, m_cur)
            m_safe = jnp.where(m_new == -jnp.inf, 0.0, m_new)
            alpha = jnp.exp2(m_prev - m_safe)
            pr = jnp.exp2(s_ref[h % 2] - m_safe)
            l_ref[h:h + 1, :] = alpha * l_ref[h:h + 1, :] + jnp.sum(pr, axis=0, keepdims=True)
            m_ref[h:h + 1, :] = m_new
            acc_ref[sl, :] = acc_ref[sl, :] * alpha + jnp.dot(
                vt_ref[sl, :], pr.astype(BF16), preferred_element_type=F32)

        m_cur = logits(0)
        for h in range(N_HEADS):
            m_next = logits(h + 1) if h + 1 < N_HEADS else None
            accumulate(h, m_cur)
            m_cur = m_next

    @pl.when(last_of[p] == 1)
    def _():
        for h in range(N_HEADS):
            sl = slice(h * HEAD_DIM, (h + 1) * HEAD_DIM)
            o_ref[:, sl] = (acc_ref[sl, :] / l_ref[h:h + 1, :]).T.astype(o_ref.dtype)


def _dsa_schedule(seq, tq, tk):
    qi_of, kb_of, last_of, nsc_of = [], [], [], []
    for qi in range(seq // tq):
        last_kb = ((qi + 1) * tq - 1) // tk
        for kb in range(last_kb + 1):
            qi_of.append(qi)
            kb_of.append(kb)
            last_of.append(int(kb == last_kb))
            nsc_of.append(-(-((qi + 1) * tq) // SCORE_ROWS))
    return tuple(np.asarray(a, np.int32) for a in (qi_of, kb_of, last_of, nsc_of))


def dsa_attention(qn, qip, wis, kil, kn, vt, bounded, batch, seq, topk, tq=256, tk=512):
    d = N_HEADS * HEAD_DIM
    nq = seq // tq
    nk = seq // tk
    sched = _dsa_schedule(seq, tq, tk)
    n_steps = len(sched[0])
    qrow = lambda w: pl.BlockSpec((tq, w), lambda b, p, qi_of, kb_of, *_: (b * nq + qi_of[p], 0))
    grid_spec = pltpu.PrefetchScalarGridSpec(
        num_scalar_prefetch=5,
        grid=(batch, n_steps),
        in_specs=[qrow(d), qrow(d), qrow(LANES),
                  pl.BlockSpec((seq, LANES), lambda b, p, *_: (b, 0)),
                  pl.BlockSpec((tk, d), lambda b, p, qi_of, kb_of, *_: (b * nk + kb_of[p], 0)),
                  pl.BlockSpec((d, tk), lambda b, p, qi_of, kb_of, *_: (b, kb_of[p]))],
        out_specs=qrow(d),
        scratch_shapes=[pltpu.VMEM((seq, tq), I32),
                        pltpu.VMEM((1, tq), I32),
                        pltpu.VMEM((N_IDX_HEADS, tq), F32),
                        pltpu.VMEM((N_HEADS, tq), F32),
                        pltpu.VMEM((N_HEADS, tq), F32),
                        pltpu.VMEM((d, tq), F32),
                        pltpu.VMEM((2, tk, tq), F32),
                        pltpu.VMEM((MAX_TOPK, tq), I32)],
    )
    return pl.pallas_call(
        functools.partial(_dsa_kernel, tq=tq, tk=tk, topk=topk),
        grid_spec=grid_spec,
        out_shape=jax.ShapeDtypeStruct((batch * seq, d), BF16),
        compiler_params=_cparams("parallel", "arbitrary"),
        name="dsa_attention",
    )(*sched, bounded, qn, qip, wis, kil, kn, vt)


LOGIT_BOUND = 100.0


def _logits_bounded(q_gain, k_gain):
    bound = HEAD_DIM * jnp.max(jnp.abs(q_gain)) * jnp.max(jnp.abs(k_gain)) * (HEAD_DIM ** -0.5 * LOG2E)
    return (bound <= LOGIT_BOUND).astype(I32).reshape(1)


def _mem_kv_kernel(mem_ref, g_ref, w_ref, kg_ref, mk_ref, mv_ref):
    d = N_MEM_HEADS * MEM_HEAD_DIM
    h = _rms(mem_ref[...], g_ref[...]).astype(BF16)
    kv = jnp.dot(h, w_ref[...], preferred_element_type=F32)
    for hd in range(N_MEM_HEADS):
        sl = slice(hd * MEM_HEAD_DIM, (hd + 1) * MEM_HEAD_DIM)
        mk_ref[:, sl] = _rms(kv[:, sl], kg_ref[...]).astype(BF16)
    mv_ref[...] = kv[:, d:].astype(BF16)


def mem_kv(mem2d, mem_norm, w_mem_kv, mk_norm, batch, n_mem):
    dm = mem2d.shape[1]
    d = N_MEM_HEADS * MEM_HEAD_DIM
    blk = pl.BlockSpec((n_mem, d), lambda b: (b, 0))
    return pl.pallas_call(
        _mem_kv_kernel,
        grid=(batch,),
        in_specs=[pl.BlockSpec((n_mem, dm), lambda b: (b, 0)),
                  pl.BlockSpec((1, dm), lambda b: (0, 0)),
                  pl.BlockSpec((dm, 2 * d), lambda b: (0, 0)),
                  pl.BlockSpec((1, MEM_HEAD_DIM), lambda b: (0, 0))],
        out_specs=[blk, blk],
        out_shape=[jax.ShapeDtypeStruct((batch * n_mem, d), BF16)] * 2,
        compiler_params=_cparams("arbitrary"),
        name="mem_kv",
    )(mem2d, mem_norm, w_mem_kv, mk_norm)


def _mem_attn_kernel(q_ref, mk_ref, mv_ref, g_ref, o_ref, *, scale):
    for hd in range(N_MEM_HEADS):
        sl = slice(hd * MEM_HEAD_DIM, (hd + 1) * MEM_HEAD_DIM)
        qh = (_rms(q_ref[:, sl], g_ref[...]) * scale).astype(BF16)
        s = lax.dot_general(qh, mk_ref[:, sl], (((1,), (1,)), ((), ())), preferred_element_type=F32)
        pr = jnp.exp2(s - jnp.max(s, axis=1, keepdims=True))
        o = jnp.dot(pr.astype(BF16), mv_ref[:, sl], preferred_element_type=F32)
        o_ref[:, sl] = (o / jnp.sum(pr, axis=1, keepdims=True)).astype(o_ref.dtype)


def mem_attention(p_main, mk, mv, mq_norm, seq, n_mem, tm):
    m = p_main.shape[0]
    d = N_MEM_HEADS * MEM_HEAD_DIM
    tiles_per_seq = seq // tm
    kv = pl.BlockSpec((n_mem, d), lambda i: (i // tiles_per_seq, 0))
    return pl.pallas_call(
        functools.partial(_mem_attn_kernel, scale=MEM_HEAD_DIM ** -0.5 * LOG2E),
        grid=(m // tm,),
        in_specs=[pl.BlockSpec((tm, d), lambda i: (i, COL_QM)), kv, kv,
                  pl.BlockSpec((1, MEM_HEAD_DIM), lambda i: (0, 0))],
        out_specs=pl.BlockSpec((tm, d), lambda i: (i, 0)),
        out_shape=jax.ShapeDtypeStruct((m, d), BF16),
        compiler_params=_cparams("parallel"),
        name="mem_attention",
    )(p_main, mk, mv, mq_norm)


HALO = 8


def _merge_kernel(x_ref, cb_ref, cc_ref, cu_ref, cch_ref, cuh_ref, gc_ref, ga_ref, gm_ref,
                  att_ref, mem_ref, cw_ref, wc_ref, wa_ref, wm_ref, wo_ref, o_ref, *, tiles_per_seq):
    i = pl.program_id(0)
    pcur = cc_ref[...] * cu_ref[...]
    halo = jnp.where(i % tiles_per_seq == 0, 0.0, cch_ref[...] * cuh_ref[...])
    row = lax.broadcasted_iota(I32, pcur.shape, 0)
    h1 = halo[HALO - 1:HALO, :]
    h2 = halo[HALO - 2:HALO - 1, :]
    p1 = jnp.where(row == 0, h1, pltpu.roll(pcur, 1, 0))
    p2 = jnp.where(row == 0, h2, jnp.where(row == 1, h1, pltpu.roll(pcur, 2, 0)))
    cw = cw_ref[...]
    conv = cb_ref[...] * (cw[0:1, :] * p2 + cw[1:2, :] * p1 + cw[2:3, :] * pcur)
    merged = jax.nn.sigmoid(gc_ref[...]) * jnp.dot(conv.astype(BF16), wc_ref[...], preferred_element_type=F32)
    merged += jax.nn.sigmoid(ga_ref[...]) * jnp.dot(att_ref[...], wa_ref[...], preferred_element_type=F32)
    merged += jax.nn.sigmoid(gm_ref[...]) * jnp.dot(mem_ref[...], wm_ref[...], preferred_element_type=F32)
    o_ref[...] = x_ref[...] + jnp.dot(merged.astype(BF16), wo_ref[...], preferred_element_type=F32)


def merge(x2d, p_main, att, memo, conv_w, w_c, w_a, w_m, w_o, seq, tm):
    m, d = x2d.shape
    tiles_per_seq = seq // tm
    col = lambda c: pl.BlockSpec((tm, d), lambda i, c=c: (i, c))
    halo = lambda c: pl.BlockSpec((HALO, d), lambda i, c=c: (jnp.maximum(i * (tm // HALO) - 1, 0), c))
    rowb = pl.BlockSpec((tm, d), lambda i: (i, 0))
    wspec = pl.BlockSpec((d, d), lambda i: (0, 0))
    return pl.pallas_call(
        functools.partial(_merge_kernel, tiles_per_seq=tiles_per_seq),
        grid=(m // tm,),
        in_specs=[rowb, col(COL_CB), col(COL_CC), col(COL_CU), halo(COL_CC), halo(COL_CU),
                  col(COL_GC), col(COL_GA), col(COL_GM), rowb, rowb,
                  pl.BlockSpec((CONV_WIDTH, d), lambda i: (0, 0)),
                  wspec, wspec, wspec, wspec],
        out_specs=rowb,
        out_shape=jax.ShapeDtypeStruct((m, d), F32),
        compiler_params=_cparams("parallel"),
        name="merge",
    )(x2d, p_main, p_main, p_main, p_main, p_main, p_main, p_main, p_main, att, memo,
      conv_w, w_c, w_a, w_m, w_o)


def _ffn_kernel(x_ref, g_ref, wg_ref, wu_ref, wd_ref, o_ref, h_ref, acc_ref):
    j = pl.program_id(1)

    @pl.when(j == 0)
    def _():
        h_ref[...] = _rms(x_ref[...], g_ref[...]).astype(BF16)
        acc_ref[...] = jnp.zeros(acc_ref.shape, F32)

    h = h_ref[...]
    a = jax.nn.silu(jnp.dot(h, wg_ref[...], preferred_element_type=F32)) * jnp.dot(
        h, wu_ref[...], preferred_element_type=F32)
    acc_ref[...] += jnp.dot(a.astype(BF16), wd_ref[...], preferred_element_type=F32)

    @pl.when(j == pl.num_programs(1) - 1)
    def _():
        o_ref[...] = x_ref[...] + acc_ref[...]


def ffn_dense(x2d, g, w_gate, w_up, w_down, tm, tf):
    m, d = x2d.shape
    ff = w_gate.shape[1]
    return pl.pallas_call(
        _ffn_kernel,
        grid=(m // tm, ff // tf),
        in_specs=[pl.BlockSpec((tm, d), lambda i, j: (i, 0)),
                  pl.BlockSpec((1, d), lambda i, j: (0, 0)),
                  pl.BlockSpec((d, tf), lambda i, j: (0, j)),
                  pl.BlockSpec((d, tf), lambda i, j: (0, j)),
                  pl.BlockSpec((tf, d), lambda i, j: (j, 0))],
        out_specs=pl.BlockSpec((tm, d), lambda i, j: (i, 0)),
        out_shape=jax.ShapeDtypeStruct((m, d), F32),
        scratch_shapes=[pltpu.VMEM((tm, d), BF16), pltpu.VMEM((tm, d), F32)],
        compiler_params=_cparams("parallel", "arbitrary"),
        name="ffn_dense",
    )(x2d, g, w_gate, w_up, w_down)


def _router_kernel(x_ref, g_ref, r_ref, h_ref, eid_ref, gw_ref):
    h = _rms(x_ref[...], g_ref[...])
    h_ref[...] = h
    logits = jnp.dot(h, r_ref[...], preferred_element_type=F32, precision=lax.Precision.HIGHEST)
    lane = lax.broadcasted_iota(I32, logits.shape, 1)
    lanef = lane.astype(F32)
    logits = jnp.where(lane < N_EXPERTS, logits, -jnp.inf)
    m1 = jnp.max(logits, axis=1, keepdims=True)
    i1 = jnp.min(jnp.where(logits == m1, lanef, float(LANES)), axis=1, keepdims=True)
    rest = jnp.where(lanef == i1, -jnp.inf, logits)
    m2 = jnp.max(rest, axis=1, keepdims=True)
    i2 = jnp.min(jnp.where(rest == m2, lanef, float(LANES)), axis=1, keepdims=True)
    e = jnp.exp(m2 - m1)
    g1 = 1.0 / (1.0 + e)
    eid_ref[...] = jnp.where(lane == 0, i1, jnp.where(lane == 1, i2, 0.0)).astype(I32)
    gw_ref[...] = jnp.where(lane == 0, g1, jnp.where(lane == 1, e * g1, 0.0))


def router(x2d, g, router_pad, tm):
    m, d = x2d.shape
    lane_out = pl.BlockSpec((tm, LANES), lambda i: (i, 0))
    return pl.pallas_call(
        _router_kernel,
        grid=(m // tm,),
        in_specs=[pl.BlockSpec((tm, d), lambda i: (i, 0)),
                  pl.BlockSpec((1, d), lambda i: (0, 0)),
                  pl.BlockSpec((d, LANES), lambda i: (0, 0))],
        out_specs=[pl.BlockSpec((tm, d), lambda i: (i, 0)), lane_out, lane_out],
        out_shape=[jax.ShapeDtypeStruct((m, d), F32),
                   jax.ShapeDtypeStruct((m, LANES), I32),
                   jax.ShapeDtypeStruct((m, LANES), F32)],
        compiler_params=_cparams("parallel"),
        name="router",
    )(x2d, g, router_pad)


def _expert_kernel(blk_exp, n_used, tok_ref, tok_next_ref, w_ref, h_ref, wg_ref, wu_ref, wd_ref, y_ref,
                   xbuf, xb_ref, acc_ref, sem, *, mb):
    b = pl.program_id(0)
    j = pl.program_id(1)
    active = b < n_used[0]
    last = j == pl.num_programs(1) - 1
    slot = b % 2

    def row_copy(idx_ref, r, s):
        return pltpu.make_async_copy(h_ref.at[pl.ds(idx_ref[0, 0, r], 1)], xbuf.at[s, pl.ds(r, 1)], sem.at[s])

    def gather(idx_ref, s):
        def start(r, c):
            row_copy(idx_ref, r, s).start()
            return c

        lax.fori_loop(0, mb, start, 0)

    @pl.when(jnp.logical_and(b == 0, j == 0))
    def _():
        gather(tok_ref, 0)

    @pl.when(jnp.logical_and(active, j == 0))
    def _():
        def wait(r, c):
            row_copy(tok_ref, r, slot).wait()
            return c

        lax.fori_loop(0, mb, wait, 0)
        xb_ref[...] = xbuf[slot].astype(BF16)
        acc_ref[...] = jnp.zeros(acc_ref.shape, F32)

    @pl.when(jnp.logical_and(b + 1 < n_used[0], j == 1))
    def _():
        gather(tok_next_ref, 1 - slot)

    @pl.when(active)
    def _():
        x = xb_ref[...]
        a = jax.nn.silu(jnp.dot(x, wg_ref[0], preferred_element_type=F32)) * jnp.dot(
            x, wu_ref[0], preferred_element_type=F32)
        acc_ref[...] += jnp.dot(a.astype(BF16), wd_ref[0], preferred_element_type=F32)

    @pl.when(jnp.logical_and(active, last))
    def _():
        y_ref[...] = acc_ref[...] * w_ref[...]

    @pl.when(jnp.logical_and(jnp.logical_not(active), last))
    def _():
        y_ref[...] = jnp.zeros(y_ref.shape, F32)


def expert_ffn(hf, slot_tok, slot_w, blk_exp, n_used, ex_gate, ex_up, ex_down, mb, tf):
    m, d = hf.shape
    n_blk = blk_exp.shape[0]
    ff = ex_gate.shape[2]
    assert ff // tf >= 2
    tok3 = slot_tok.reshape(n_blk, 1, mb)
    grid_spec = pltpu.PrefetchScalarGridSpec(
        num_scalar_prefetch=2,
        grid=(n_blk, ff // tf),
        in_specs=[pl.BlockSpec((1, 1, mb), lambda b, j, be, nu: (b, 0, 0), memory_space=pltpu.SMEM),
                  pl.BlockSpec((1, 1, mb), lambda b, j, be, nu: (jnp.minimum(b + 1, n_blk - 1), 0, 0),
                               memory_space=pltpu.SMEM),
                  pl.BlockSpec((mb, 1), lambda b, j, be, nu: (b, 0)),
                  pl.BlockSpec(memory_space=pl.ANY),
                  pl.BlockSpec((1, d, tf), lambda b, j, be, nu: (be[b], 0, j)),
                  pl.BlockSpec((1, d, tf), lambda b, j, be, nu: (be[b], 0, j)),
                  pl.BlockSpec((1, tf, d), lambda b, j, be, nu: (be[b], j, 0))],
        out_specs=pl.BlockSpec((mb, d), lambda b, j, be, nu: (b, 0)),
        scratch_shapes=[pltpu.VMEM((2, mb, d), F32), pltpu.VMEM((mb, d), BF16), pltpu.VMEM((mb, d), F32),
                        pltpu.SemaphoreType.DMA((2,))],
    )
    return pl.pallas_call(
        functools.partial(_expert_kernel, mb=mb),
        grid_spec=grid_spec,
        out_shape=jax.ShapeDtypeStruct((n_blk * mb, d), F32),
        compiler_params=_cparams("arbitrary", "arbitrary"),
        name="expert_ffn",
    )(blk_exp, n_used, tok3, tok3, slot_w.reshape(n_blk * mb, 1), hf, ex_gate, ex_up, ex_down)


def _combine_kernel(pos_ref, pos_next_ref, x_ref, ys_ref, o_ref, buf, sem, *, tm):
    i = pl.program_id(0)
    slot = i % 2
    n_rows = TOP_K * tm

    def row_copy(idx_ref, r, s):
        return pltpu.make_async_copy(ys_ref.at[pl.ds(idx_ref[0, 0, r], 1)], buf.at[s, pl.ds(r, 1)], sem.at[s])

    def gather(idx_ref, s):
        def start(r, c):
            row_copy(idx_ref, r, s).start()
            return c

        lax.fori_loop(0, n_rows, start, 0)

    @pl.when(i == 0)
    def _():
        gather(pos_ref, 0)

    @pl.when(i + 1 < pl.num_programs(0))
    def _():
        gather(pos_next_ref, 1 - slot)

    def wait(r, c):
        row_copy(pos_ref, r, slot).wait()
        return c

    lax.fori_loop(0, n_rows, wait, 0)
    out = x_ref[...]
    for kk in range(TOP_K):
        out = out + buf[slot, kk * tm:(kk + 1) * tm, :]
    o_ref[...] = out


def moe_combine(x2d, ys, pos, tm):
    m, d = x2d.shape
    n_tiles = m // tm
    return pl.pallas_call(
        functools.partial(_combine_kernel, tm=tm),
        grid=(n_tiles,),
        in_specs=[pl.BlockSpec((1, 1, TOP_K * tm), lambda i: (i, 0, 0), memory_space=pltpu.SMEM),
                  pl.BlockSpec((1, 1, TOP_K * tm), lambda i: (jnp.minimum(i + 1, n_tiles - 1), 0, 0),
                               memory_space=pltpu.SMEM),
                  pl.BlockSpec((tm, d), lambda i: (i, 0)),
                  pl.BlockSpec(memory_space=pl.ANY)],
        out_specs=pl.BlockSpec((tm, d), lambda i: (i, 0)),
        out_shape=jax.ShapeDtypeStruct((m, d), F32),
        scratch_shapes=[pltpu.VMEM((2, TOP_K * tm, d), F32), pltpu.SemaphoreType.DMA((2,))],
        compiler_params=_cparams("arbitrary"),
        name="moe_combine",
    )(pos, pos, x2d, ys)


def _moe_plan(eid, gw, n_tok, mb):
    n_asg = n_tok * TOP_K
    n_blk = n_asg // mb + N_EXPERTS
    cap = n_blk * mb
    e = eid.reshape(-1)
    tok = jnp.repeat(jnp.arange(n_tok, dtype=I32), TOP_K)
    onehot = (e[:, None] == jnp.arange(N_EXPERTS, dtype=I32)[None, :]).astype(I32)
    rank = jnp.cumsum(onehot, axis=0) - onehot
    rank = jnp.sum(rank * onehot, axis=1)
    counts = jnp.sum(onehot, axis=0)
    padded = (counts + mb - 1) // mb * mb
    pad_end = jnp.cumsum(padded)
    pad_start = pad_end - padded
    dest = (pad_start[e] + rank).astype(I32)
    slot_tok = jnp.zeros((cap,), I32).at[dest].set(tok)
    slot_w = jnp.zeros((cap,), F32).at[dest].set(gw.reshape(-1))
    blk_start = jnp.arange(n_blk, dtype=I32) * mb
    blk_exp = jnp.minimum(jnp.searchsorted(pad_end, blk_start, side='right'), N_EXPERTS - 1).astype(I32)
    n_used = (pad_end[-1] // mb).astype(I32).reshape(1)
    return slot_tok, slot_w, blk_exp, n_used, dest.reshape(n_tok, TOP_K)


def moe_layer(x2d, g, router_w, ex_gate, ex_up, ex_down, mb=512, tf=896, tm_route=512, tm_comb=256):
    n_tok, d = x2d.shape
    router_pad = jnp.zeros((d, LANES), F32).at[:, :N_EXPERTS].set(router_w)
    hf, eid, gw = router(x2d, g, router_pad, tm_route)
    slot_tok, slot_w, blk_exp, n_used, dest = _moe_plan(eid[:, :TOP_K], gw[:, :TOP_K], n_tok, mb)
    ys = expert_ffn(hf, slot_tok, slot_w, blk_exp, n_used, ex_gate, ex_up, ex_down, mb, tf)
    pos = dest.reshape(n_tok // tm_comb, tm_comb, TOP_K).transpose(0, 2, 1).reshape(
        n_tok // tm_comb, 1, TOP_K * tm_comb)
    return moe_combine(x2d, ys, pos, tm_comb)


def _rope_tables(seq):
    def tab(dim):
        inv = 1.0 / (ROPE_THETA ** (jnp.arange(0, dim, 2, dtype=F32) / dim))
        ang = jnp.arange(seq, dtype=F32)[:, None] * inv[None, :]
        cos, sin = jnp.cos(ang), jnp.sin(ang)
        reps = LANES // dim
        return (jnp.tile(jnp.concatenate([cos, cos], axis=1), (1, reps)),
                jnp.tile(jnp.concatenate([-sin, sin], axis=1), (1, reps)))

    ca, sa = tab(HEAD_DIM)
    ci, si = tab(IDX_DIM)
    return ca, sa, ci, si


def _pack_w_in(w):
    d_blk = N_HEADS * HEAD_DIM
    n_idx = N_IDX_HEADS * IDX_DIM + IDX_DIM + N_IDX_HEADS
    v_lo, v_hi = 5 * d_blk, 6 * d_blk
    main = jnp.concatenate([w[:, :v_lo], w[:, v_hi + n_idx:]], axis=1)
    idx = jnp.pad(w[:, v_hi:v_hi + n_idx], ((0, 0), (0, IDX_COLS - n_idx)))
    return main.astype(BF16), w[:, v_lo:v_hi].T.astype(BF16), idx.astype(BF16)


def kernel(x, mem, attn_norm, w_in, conv_w, q_norm, k_norm, mem_norm, w_mem_kv, mq_norm, mk_norm, w_br_conv, w_br_attn, w_br_mem, w_out, ffn_norm, ff_gate, ff_up, ff_down, router, ex_gate, ex_up, ex_down):
    batch, seq, d = x.shape
    depth = attn_norm.shape[0]
    n_mem = mem.shape[1]
    topk = min(MAX_TOPK, seq // 4)
    m = batch * seq
    rope = _rope_tables(seq)
    mem2d = mem.reshape(batch * n_mem, d)
    xc = x.reshape(m, d)
    row = lambda a: a.reshape(1, -1)
    for l in range(depth):
        w_main, w_vt, w_idx = _pack_w_in(w_in[l])
        g = row(attn_norm[l])
        p_main = norm_matmul(xc, g, w_main, tm=1024, tn=1024)
        p_idx = norm_matmul(xc, g, w_idx, tm=1024, tn=IDX_COLS)
        vt = norm_matmul_t(xc, g, w_vt, seq, tm=512)
        qn, kn, qip, kil, wis = dsa_prep(p_main, p_idx, rope, row(q_norm[l]), row(k_norm[l]), seq, ts=512)
        att = dsa_attention(qn, qip, wis, kil, kn, vt, _logits_bounded(q_norm[l], k_norm[l]),
                            batch, seq, topk)
        mk, mv = mem_kv(mem2d, row(mem_norm[l]), w_mem_kv[l].astype(BF16), row(mk_norm[l]), batch, n_mem)
        memo = mem_attention(p_main, mk, mv, row(mq_norm[l]), seq, n_mem, tm=512)
        xc = merge(xc, p_main, att, memo, conv_w[l], w_br_conv[l].astype(BF16), w_br_attn[l].astype(BF16),
                   w_br_mem[l].astype(BF16), w_out[l].astype(BF16), seq, tm=256)
        j = l // 2
        if l % 2 == 0:
            xc = ffn_dense(xc, row(ffn_norm[l]), ff_gate[j].astype(BF16), ff_up[j].astype(BF16),
                           ff_down[j].astype(BF16), tm=512, tf=512)
        else:
            xc = moe_layer(xc, row(ffn_norm[l]), router[j], ex_gate[j].astype(BF16), ex_up[j].astype(BF16),
                           ex_down[j].astype(BF16))
    return xc.reshape(batch, seq, d)
```

```python
import functools
import math

import numpy as np
import jax
import jax.numpy as jnp
from jax import lax
from jax.experimental import pallas as pl
from jax.experimental.pallas import tpu as pltpu

F32 = jnp.float32
BF16 = jnp.bfloat16
I32 = jnp.int32

N_HEADS = 8
HEAD_DIM = 128
N_IDX_HEADS = 8
IDX_DIM = 64
MAX_TOPK = 256
N_MEM_HEADS = 4
MEM_HEAD_DIM = 256
N_EXPERTS = 8
TOP_K = 2
CONV_WIDTH = 3
ROPE_THETA = 10000.0
EPS = 1e-6

LANES = 128
SUBLANES = 8
INT_MIN = -(2 ** 31)
INT_MAX = 2 ** 31 - 1
LOG2E = math.log2(math.e)
VMEM_LIMIT = 56 * 1024 * 1024

COL_CB, COL_CC, COL_CU, COL_Q, COL_K, COL_QM, COL_GC, COL_GA, COL_GM = range(9)
IDX_COLS = 640


def _cparams(*sem):
    return pltpu.CompilerParams(dimension_semantics=sem, vmem_limit_bytes=VMEM_LIMIT)


def _rms(x, g):
    return x * lax.rsqrt(jnp.mean(x * x, axis=-1, keepdims=True) + EPS) * g


def _norm_matmul_kernel(x_ref, g_ref, w_ref, o_ref, h_ref):
    @pl.when(pl.program_id(1) == 0)
    def _():
        h_ref[...] = _rms(x_ref[...], g_ref[...]).astype(BF16)

    o_ref[...] = jnp.dot(h_ref[...], w_ref[...], preferred_element_type=F32)


def norm_matmul(x, g, w, tm, tn):
    m, k = x.shape
    n = w.shape[1]
    return pl.pallas_call(
        _norm_matmul_kernel,
        grid=(m // tm, n // tn),
        in_specs=[pl.BlockSpec((tm, k), lambda i, j: (i, 0)),
                  pl.BlockSpec((1, k), lambda i, j: (0, 0)),
                  pl.BlockSpec((k, tn), lambda i, j: (0, j))],
        out_specs=pl.BlockSpec((tm, tn), lambda i, j: (i, j)),
        out_shape=jax.ShapeDtypeStruct((m, n), F32),
        scratch_shapes=[pltpu.VMEM((tm, k), BF16)],
        compiler_params=_cparams("parallel", "arbitrary"),
        name="norm_matmul",
    )(x, g, w)


def _norm_matmul_t_kernel(x_ref, g_ref, wt_ref, o_ref):
    h = _rms(x_ref[...], g_ref[...]).astype(BF16)
    o_ref[...] = lax.dot_general(wt_ref[...], h, (((1,), (1,)), ((), ())),
                                 preferred_element_type=F32).astype(o_ref.dtype)


def norm_matmul_t(x, g, wt, seq, tm):
    m, k = x.shape
    n = wt.shape[0]
    tiles_per_seq = seq // tm
    return pl.pallas_call(
        _norm_matmul_t_kernel,
        grid=(m // tm,),
        in_specs=[pl.BlockSpec((tm, k), lambda i: (i, 0)),
                  pl.BlockSpec((1, k), lambda i: (0, 0)),
                  pl.BlockSpec((n, k), lambda i: (0, 0))],
        out_specs=pl.BlockSpec((n, tm), lambda i: (i // tiles_per_seq, i % tiles_per_seq)),
        out_shape=jax.ShapeDtypeStruct((m // seq * n, seq), BF16),
        compiler_params=_cparams("parallel"),
        name="norm_matmul_t",
    )(x, g, wt)


def _prep_kernel(q_ref, k_ref, idx_ref, ca_ref, sa_ref, ci_ref, si_ref, qg_ref, kg_ref,
                 qn_ref, kn_ref, qip_ref, kil_ref, wis_ref, *, q_scale, w_scale):
    ca = ca_ref[...]
    sa = sa_ref[...]
    for h in range(N_HEADS):
        sl = slice(h * HEAD_DIM, (h + 1) * HEAD_DIM)
        for src, g_ref, dst, scale in ((q_ref, qg_ref, qn_ref, q_scale), (k_ref, kg_ref, kn_ref, 1.0)):
            n = _rms(src[:, sl], g_ref[...])
            o = n * ca + pltpu.roll(n, HEAD_DIM // 2, 1) * sa
            dst[:, sl] = (o * scale).astype(BF16)

    ci = ci_ref[...]
    si = si_ref[...]
    lane = lax.broadcasted_iota(I32, ci.shape, 1)
    first_half = (lane & (IDX_DIM // 2)) == 0
    low_head = lane < IDX_DIM

    def idx_rope(x):
        rot = jnp.where(first_half, pltpu.roll(x, LANES - IDX_DIM // 2, 1), pltpu.roll(x, IDX_DIM // 2, 1))
        return x * ci + rot * si

    for c in range(N_IDX_HEADS // 2):
        o = idx_rope(idx_ref[:, c * LANES:(c + 1) * LANES])
        qip_ref[:, (2 * c) * LANES:(2 * c + 1) * LANES] = jnp.where(low_head, o, 0.0).astype(BF16)
        qip_ref[:, (2 * c + 1) * LANES:(2 * c + 2) * LANES] = jnp.where(
            low_head, pltpu.roll(o, IDX_DIM, 1), 0.0).astype(BF16)
    tail = idx_ref[:, N_IDX_HEADS * IDX_DIM:N_IDX_HEADS * IDX_DIM + LANES]
    kil_ref[...] = jnp.where(low_head, idx_rope(tail), 0.0).astype(BF16)
    wis_ref[...] = tail * w_scale


def dsa_prep(p_main, p_idx, rope, q_norm, k_norm, seq, ts):
    m = p_main.shape[0]
    d = N_HEADS * HEAD_DIM
    n_seq_tiles = seq // ts
    ca, sa, ci, si = rope
    row = lambda c: pl.BlockSpec((ts, d), lambda i, c=c: (i, c))
    tab = pl.BlockSpec((ts, LANES), lambda i: (i % n_seq_tiles, 0))
    gain = pl.BlockSpec((1, HEAD_DIM), lambda i: (0, 0))
    out_d = pl.BlockSpec((ts, d), lambda i: (i, 0))
    out_l = pl.BlockSpec((ts, LANES), lambda i: (i, 0))
    q_scale = HEAD_DIM ** -0.5 * LOG2E
    w_scale = (N_IDX_HEADS * IDX_DIM) ** -0.5
    return pl.pallas_call(
        functools.partial(_prep_kernel, q_scale=q_scale, w_scale=w_scale),
        grid=(m // ts,),
        in_specs=[row(COL_Q), row(COL_K),
                  pl.BlockSpec((ts, IDX_COLS), lambda i: (i, 0)),
                  tab, tab, tab, tab, gain, gain],
        out_specs=[out_d, out_d, out_d, out_l, out_l],
        out_shape=[jax.ShapeDtypeStruct((m, d), BF16)] * 3
        + [jax.ShapeDtypeStruct((m, LANES), BF16), jax.ShapeDtypeStruct((m, LANES), F32)],
        compiler_params=_cparams("parallel"),
        name="dsa_prep",
    )(p_main, p_main, p_idx, ca, sa, ci, si, q_norm, k_norm)


SCORE_ROWS = 512
SLAB = 32
NT_DIMS = (((1,), (1,)), ((), ()))
ZERO_SCORE_BASE = 2 ** 23 - 1


def _dsa_kernel(qi_of, kb_of, last_of, nsc_of, bounded,
                qn_ref, qip_ref, wis_ref, kil_ref, kn_ref, vt_ref,
                o_ref,
                keys_ref, thr_ref, w_ref, m_ref, l_ref, acc_ref, s_ref, gmax_ref,
                *, tq, tk, topk):
    p = pl.program_id(1)
    qi = qi_of[p]
    kb = kb_of[p]

    @pl.when(kb == 0)
    def _():
        w_ref[...] = wis_ref[...].T[IDX_DIM:IDX_DIM + N_IDX_HEADS, :]
        q_id = qi * tq + lax.broadcasted_iota(I32, (SCORE_ROWS, tq), 1)
        k_iota = lax.broadcasted_iota(I32, (SCORE_ROWS, tq), 0)
        n_sc = nsc_of[p]

        def score_step(c, carry):
            start = pl.multiple_of(c * SCORE_ROWS, SCORE_ROWS)
            kblk = kil_ref[pl.ds(start, SCORE_ROWS), :]
            sc = jnp.zeros((SCORE_ROWS, tq), F32)
            for h in range(N_IDX_HEADS):
                z = lax.dot_general(kblk, qip_ref[:, h * LANES:(h + 1) * LANES], NT_DIMS,
                                    preferred_element_type=F32)
                sc = sc + jnp.maximum(z, 0.0) * w_ref[h:h + 1, :]
            bits = pltpu.bitcast(sc, I32)
            key = jnp.where(bits < 0, bits ^ 0x7FFFFFFF, bits)
            k_id = k_iota + start
            key = jnp.where(key == 0, ZERO_SCORE_BASE - k_id, key)
            key = jnp.where(k_id <= q_id, key, INT_MIN)
            keys_ref[pl.ds(start, SCORE_ROWS), :] = key
            gmax = gmax_ref[...]
            for j in range(SCORE_ROWS // MAX_TOPK):
                gmax = jnp.maximum(gmax, key[j * MAX_TOPK:(j + 1) * MAX_TOPK, :])
            gmax_ref[...] = gmax
            return carry

        gmax_ref[...] = jnp.full(gmax_ref.shape, INT_MIN, I32)
        lax.fori_loop(0, n_sc, score_step, 0)

        def count_ge(cand):
            def count_step(c, acc):
                start = pl.multiple_of(c * SCORE_ROWS, SCORE_ROWS)
                for j in range(SCORE_ROWS // SLAB):
                    acc = acc + (keys_ref[pl.ds(start + j * SLAB, SLAB), :] >= cand).astype(I32)
                return acc

            acc = lax.fori_loop(0, n_sc, count_step, jnp.zeros((SLAB, tq), I32))
            return jnp.sum(acc.astype(F32), axis=0, keepdims=True)

        gmax = gmax_ref[...]
        lo0 = jnp.maximum(jnp.min(gmax, axis=0, keepdims=True), INT_MIN + 1)
        hi0 = jnp.minimum(jnp.max(gmax, axis=0, keepdims=True), INT_MAX - 1) + 1

        def open_interval(lo, hi):
            return jnp.max(((lo + 1) < hi).astype(I32)) > 0

        def bisect(state):
            lo, hi, _ = state
            mid = (lo & hi) + ((lo ^ hi) >> 1)
            cnt = count_ge(mid)
            keep = cnt >= topk
            lo = jnp.where(keep, mid, lo)
            hi = jnp.where(cnt == topk, mid + 1, jnp.where(keep, hi, mid))
            return lo, hi, open_interval(lo, hi)

        lo, _, _ = lax.while_loop(lambda st: st[2], bisect, (lo0, hi0, open_interval(lo0, hi0)))
        thr_ref[...] = lo

        m_ref[...] = jnp.full(m_ref.shape, -jnp.inf, F32)
        l_ref[...] = jnp.zeros(l_ref.shape, F32)
        acc_ref[...] = jnp.zeros(acc_ref.shape, F32)

    def mask_bias():
        keys = keys_ref[pl.ds(pl.multiple_of(kb * tk, tk), tk), :]
        return jnp.where(keys >= thr_ref[...], 0.0, -jnp.inf).astype(F32)

    def masked_logits(h, bias):
        sl = slice(h * HEAD_DIM, (h + 1) * HEAD_DIM)
        return lax.dot_general(kn_ref[:, sl], qn_ref[:, sl], NT_DIMS, preferred_element_type=F32) + bias


    @pl.when(bounded[0] == 1)
    def _():
        bias = mask_bias()
        s_ref[0] = masked_logits(0, bias)
        for h in range(N_HEADS):
            sl = slice(h * HEAD_DIM, (h + 1) * HEAD_DIM)
            if h + 1 < N_HEADS:
                s_ref[(h + 1) % 2] = masked_logits(h + 1, bias)
            pr = jnp.exp2(s_ref[h % 2])
            l_ref[h:h + 1, :] = l_ref[h:h + 1, :] + jnp.sum(pr, axis=0, keepdims=True)
            acc_ref[sl, :] = acc_ref[sl, :] + jnp.dot(
                vt_ref[sl, :], pr.astype(BF16), preferred_element_type=F32)

    @pl.when(bounded[0] == 0)
    def _():
        bias = mask_bias()

        def logits(h):
            s = masked_logits(h, bias)
            s_ref[h % 2] = s
            return jnp.max(s, axis=0, keepdims=True)

        def accumulate(h, m_cur):
            sl = slice(h * HEAD_DIM, (h + 1) * HEAD_DIM)
            m_prev = m_ref[h:h + 1, :]
            m_new = jnp.maximum(m_prev, m_cur)
            m_safe = jnp.where(m_new == -jnp.inf, 0.0, m_new)
            alpha = jnp.exp2(m_prev - m_safe)
            pr = jnp.exp2(s_ref[h % 2] - m_safe)
            l_ref[h:h + 1, :] = alpha * l_ref[h:h + 1, :] + jnp.sum(pr, axis=0, keepdims=True)
            m_ref[h:h + 1, :] = m_new
            acc_ref[sl, :] = acc_ref[sl, :] * alpha + jnp.dot(
                vt_ref[sl, :], pr.astype(BF16), preferred_element_type=F32)

        m_cur = logits(0)
        for h in range(N_HEADS):
            m_next = logits(h + 1) if h + 1 < N_HEADS else None
            accumulate(h, m_cur)
            m_cur = m_next

    @pl.when(last_of[p] == 1)
    def _():
        for h in range(N_HEADS):
            sl = slice(h * HEAD_DIM, (h + 1) * HEAD_DIM)
            o_ref[:, sl] = (acc_ref[sl, :] / l_ref[h:h + 1, :]).T.astype(o_ref.dtype)


def _dsa_schedule(seq, tq, tk):
    qi_of, kb_of, last_of, nsc_of = [], [], [], []
    for qi in range(seq // tq):
        last_kb = ((qi + 1) * tq - 1) // tk
        for kb in range(last_kb + 1):
            qi_of.append(qi)
            kb_of.append(kb)
            last_of.append(int(kb == last_kb))
            nsc_of.append(-(-((qi + 1) * tq) // SCORE_ROWS))
    return tuple(np.asarray(a, np.int32) for a in (qi_of, kb_of, last_of, nsc_of))


def dsa_attention(qn, qip, wis, kil, kn, vt, bounded, batch, seq, topk, tq=256, tk=512):
    d = N_HEADS * HEAD_DIM
    nq = seq // tq
    nk = seq // tk
    assert seq <= ZERO_SCORE_BASE and topk <= MAX_TOPK
    sched = _dsa_schedule(seq, tq, tk)
    n_steps = len(sched[0])
    qrow = lambda w: pl.BlockSpec((tq, w), lambda b, p, qi_of, kb_of, *_: (b * nq + qi_of[p], 0))
    grid_spec = pltpu.PrefetchScalarGridSpec(
        num_scalar_prefetch=5,
        grid=(batch, n_steps),
        in_specs=[qrow(d), qrow(d), qrow(LANES),
                  pl.BlockSpec((seq, LANES), lambda b, p, *_: (b, 0)),
                  pl.BlockSpec((tk, d), lambda b, p, qi_of, kb_of, *_: (b * nk + kb_of[p], 0)),
                  pl.BlockSpec((d, tk), lambda b, p, qi_of, kb_of, *_: (b, kb_of[p]))],
        out_specs=qrow(d),
        scratch_shapes=[pltpu.VMEM((seq, tq), I32),
                        pltpu.VMEM((1, tq), I32),
                        pltpu.VMEM((N_IDX_HEADS, tq), F32),
                        pltpu.VMEM((N_HEADS, tq), F32),
                        pltpu.VMEM((N_HEADS, tq), F32),
                        pltpu.VMEM((d, tq), F32),
                        pltpu.VMEM((2, tk, tq), F32),
                        pltpu.VMEM((MAX_TOPK, tq), I32)],
    )
    return pl.pallas_call(
        functools.partial(_dsa_kernel, tq=tq, tk=tk, topk=topk),
        grid_spec=grid_spec,
        out_shape=jax.ShapeDtypeStruct((batch * seq, d), BF16),
        compiler_params=_cparams("parallel", "arbitrary"),
        name="dsa_attention",
    )(*sched, bounded, qn, qip, wis, kil, kn, vt)


LOGIT_BOUND = 100.0


def _logits_bounded(q_gain, k_gain):
    bound = HEAD_DIM * jnp.max(jnp.abs(q_gain)) * jnp.max(jnp.abs(k_gain)) * (HEAD_DIM ** -0.5 * LOG2E)
    return (bound <= LOGIT_BOUND).astype(I32).reshape(1)


def _mem_kv_kernel(mem_ref, g_ref, w_ref, kg_ref, mk_ref, mv_ref):
    d = N_MEM_HEADS * MEM_HEAD_DIM
    h = _rms(mem_ref[...], g_ref[...]).astype(BF16)
    kv = jnp.dot(h, w_ref[...], preferred_element_type=F32)
    for hd in range(N_MEM_HEADS):
        sl = slice(hd * MEM_HEAD_DIM, (hd + 1) * MEM_HEAD_DIM)
        mk_ref[:, sl] = _rms(kv[:, sl], kg_ref[...]).astype(BF16)
    mv_ref[...] = kv[:, d:].astype(BF16)


def mem_kv(mem2d, mem_norm, w_mem_kv, mk_norm, batch, n_mem):
    dm = mem2d.shape[1]
    d = N_MEM_HEADS * MEM_HEAD_DIM
    blk = pl.BlockSpec((n_mem, d), lambda b: (b, 0))
    return pl.pallas_call(
        _mem_kv_kernel,
        grid=(batch,),
        in_specs=[pl.BlockSpec((n_mem, dm), lambda b: (b, 0)),
                  pl.BlockSpec((1, dm), lambda b: (0, 0)),
                  pl.BlockSpec((dm, 2 * d), lambda b: (0, 0)),
                  pl.BlockSpec((1, MEM_HEAD_DIM), lambda b: (0, 0))],
        out_specs=[blk, blk],
        out_shape=[jax.ShapeDtypeStruct((batch * n_mem, d), BF16)] * 2,
        compiler_params=_cparams("arbitrary"),
        name="mem_kv",
    )(mem2d, mem_norm, w_mem_kv, mk_norm)


def _mem_attn_kernel(q_ref, mk_ref, mv_ref, g_ref, o_ref, *, scale):
    for hd in range(N_MEM_HEADS):
        sl = slice(hd * MEM_HEAD_DIM, (hd + 1) * MEM_HEAD_DIM)
        qh = (_rms(q_ref[:, sl], g_ref[...]) * scale).astype(BF16)
        s = lax.dot_general(qh, mk_ref[:, sl], (((1,), (1,)), ((), ())), preferred_element_type=F32)
        pr = jnp.exp2(s - jnp.max(s, axis=1, keepdims=True))
        o = jnp.dot(pr.astype(BF16), mv_ref[:, sl], preferred_element_type=F32)
        o_ref[:, sl] = (o / jnp.sum(pr, axis=1, keepdims=True)).astype(o_ref.dtype)


def mem_attention(p_main, mk, mv, mq_norm, seq, n_mem, tm):
    m = p_main.shape[0]
    d = N_MEM_HEADS * MEM_HEAD_DIM
    tiles_per_seq = seq // tm
    kv = pl.BlockSpec((n_mem, d), lambda i: (i // tiles_per_seq, 0))
    return pl.pallas_call(
        functools.partial(_mem_attn_kernel, scale=MEM_HEAD_DIM ** -0.5 * LOG2E),
        grid=(m // tm,),
        in_specs=[pl.BlockSpec((tm, d), lambda i: (i, COL_QM)), kv, kv,
                  pl.BlockSpec((1, MEM_HEAD_DIM), lambda i: (0, 0))],
        out_specs=pl.BlockSpec((tm, d), lambda i: (i, 0)),
        out_shape=jax.ShapeDtypeStruct((m, d), BF16),
        compiler_params=_cparams("parallel"),
        name="mem_attention",
    )(p_main, mk, mv, mq_norm)


HALO = 8


def _merge_kernel(x_ref, cb_ref, cc_ref, cu_ref, cch_ref, cuh_ref, gc_ref, ga_ref, gm_ref,
                  att_ref, mem_ref, cw_ref, wc_ref, wa_ref, wm_ref, wo_ref, o_ref, *, tiles_per_seq):
    i = pl.program_id(0)
    pcur = cc_ref[...] * cu_ref[...]
    halo = jnp.where(i % tiles_per_seq == 0, 0.0, cch_ref[...] * cuh_ref[...])
    row = lax.broadcasted_iota(I32, pcur.shape, 0)
    h1 = halo[HALO - 1:HALO, :]
    h2 = halo[HALO - 2:HALO - 1, :]
    p1 = jnp.where(row == 0, h1, pltpu.roll(pcur, 1, 0))
    p2 = jnp.where(row == 0, h2, jnp.where(row == 1, h1, pltpu.roll(pcur, 2, 0)))
    cw = cw_ref[...]
    conv = cb_ref[...] * (cw[0:1, :] * p2 + cw[1:2, :] * p1 + cw[2:3, :] * pcur)
    merged = jax.nn.sigmoid(gc_ref[...]) * jnp.dot(conv.astype(BF16), wc_ref[...], preferred_element_type=F32)
    merged += jax.nn.sigmoid(ga_ref[...]) * jnp.dot(att_ref[...], wa_ref[...], preferred_element_type=F32)
    merged += jax.nn.sigmoid(gm_ref[...]) * jnp.dot(mem_ref[...], wm_ref[...], preferred_element_type=F32)
    o_ref[...] = x_ref[...] + jnp.dot(merged.astype(BF16), wo_ref[...], preferred_element_type=F32)


def merge(x2d, p_main, att, memo, conv_w, w_c, w_a, w_m, w_o, seq, tm):
    m, d = x2d.shape
    tiles_per_seq = seq // tm
    col = lambda c: pl.BlockSpec((tm, d), lambda i, c=c: (i, c))
    halo = lambda c: pl.BlockSpec((HALO, d), lambda i, c=c: (jnp.maximum(i * (tm // HALO) - 1, 0), c))
    rowb = pl.BlockSpec((tm, d), lambda i: (i, 0))
    wspec = pl.BlockSpec((d, d), lambda i: (0, 0))
    return pl.pallas_call(
        functools.partial(_merge_kernel, tiles_per_seq=tiles_per_seq),
        grid=(m // tm,),
        in_specs=[rowb, col(COL_CB), col(COL_CC), col(COL_CU), halo(COL_CC), halo(COL_CU),
                  col(COL_GC), col(COL_GA), col(COL_GM), rowb, rowb,
                  pl.BlockSpec((CONV_WIDTH, d), lambda i: (0, 0)),
                  wspec, wspec, wspec, wspec],
        out_specs=rowb,
        out_shape=jax.ShapeDtypeStruct((m, d), F32),
        compiler_params=_cparams("parallel"),
        name="merge",
    )(x2d, p_main, p_main, p_main, p_main, p_main, p_main, p_main, p_main, att, memo,
      conv_w, w_c, w_a, w_m, w_o)


def _ffn_kernel(x_ref, g_ref, wg_ref, wu_ref, wd_ref, o_ref, h_ref, acc_ref):
    j = pl.program_id(1)

    @pl.when(j == 0)
    def _():
        h_ref[...] = _rms(x_ref[...], g_ref[...]).astype(BF16)
        acc_ref[...] = jnp.zeros(acc_ref.shape, F32)

    h = h_ref[...]
    a = jax.nn.silu(jnp.dot(h, wg_ref[...], preferred_element_type=F32)) * jnp.dot(
        h, wu_ref[...], preferred_element_type=F32)
    acc_ref[...] += jnp.dot(a.astype(BF16), wd_ref[...], preferred_element_type=F32)

    @pl.when(j == pl.num_programs(1) - 1)
    def _():
        o_ref[...] = x_ref[...] + acc_ref[...]


def ffn_dense(x2d, g, w_gate, w_up, w_down, tm, tf):
    m, d = x2d.shape
    ff = w_gate.shape[1]
    return pl.pallas_call(
        _ffn_kernel,
        grid=(m // tm, ff // tf),
        in_specs=[pl.BlockSpec((tm, d), lambda i, j: (i, 0)),
                  pl.BlockSpec((1, d), lambda i, j: (0, 0)),
                  pl.BlockSpec((d, tf), lambda i, j: (0, j)),
                  pl.BlockSpec((d, tf), lambda i, j: (0, j)),
                  pl.BlockSpec((tf, d), lambda i, j: (j, 0))],
        out_specs=pl.BlockSpec((tm, d), lambda i, j: (i, 0)),
        out_shape=jax.ShapeDtypeStruct((m, d), F32),
        scratch_shapes=[pltpu.VMEM((tm, d), BF16), pltpu.VMEM((tm, d), F32)],
        compiler_params=_cparams("parallel", "arbitrary"),
        name="ffn_dense",
    )(x2d, g, w_gate, w_up, w_down)


def _router_kernel(x_ref, g_ref, r_ref, h_ref, eid_ref, gw_ref):
    h = _rms(x_ref[...], g_ref[...])
    _rows_to_tiles(h_ref, h)
    logits = jnp.dot(h, r_ref[...], preferred_element_type=F32, precision=lax.Precision.HIGHEST)
    lane = lax.broadcasted_iota(I32, logits.shape, 1)
    lanef = lane.astype(F32)
    logits = jnp.where(lane < N_EXPERTS, logits, -jnp.inf)
    m1 = jnp.max(logits, axis=1, keepdims=True)
    i1 = jnp.min(jnp.where(logits == m1, lanef, float(LANES)), axis=1, keepdims=True)
    rest = jnp.where(lanef == i1, -jnp.inf, logits)
    m2 = jnp.max(rest, axis=1, keepdims=True)
    i2 = jnp.min(jnp.where(rest == m2, lanef, float(LANES)), axis=1, keepdims=True)
    e = jnp.exp(m2 - m1)
    g1 = 1.0 / (1.0 + e)
    eid_ref[...] = jnp.where(lane == 0, i1, jnp.where(lane == 1, i2, 0.0)).astype(I32)
    gw_ref[...] = jnp.where(lane == 0, g1, jnp.where(lane == 1, e * g1, 0.0))


def router(x2d, g, router_pad, tm):
    m, d = x2d.shape
    assert d == SUBLANES * LANES
    lane_out = pl.BlockSpec((tm, LANES), lambda i: (i, 0))
    return pl.pallas_call(
        _router_kernel,
        grid=(m // tm,),
        in_specs=[pl.BlockSpec((tm, d), lambda i: (i, 0)),
                  pl.BlockSpec((1, d), lambda i: (0, 0)),
                  pl.BlockSpec((d, LANES), lambda i: (0, 0))],
        out_specs=[pl.BlockSpec((tm * SUBLANES, LANES), lambda i: (i, 0)), lane_out, lane_out],
        out_shape=[jax.ShapeDtypeStruct((m * SUBLANES, LANES), F32),
                   jax.ShapeDtypeStruct((m, LANES), I32),
                   jax.ShapeDtypeStruct((m, LANES), F32)],
        compiler_params=_cparams("parallel"),
        name="router",
    )(x2d, g, router_pad)


GATHER_UNROLL = 8


def _rows_to_tiles(ref, val):
    n = val.shape[0]
    for c in range(SUBLANES):
        ref[pl.ds(c, n, stride=SUBLANES), :] = val[:, c * LANES:(c + 1) * LANES]


def _tiles_to_rows(ref, first, n):
    return jnp.concatenate(
        [ref[pl.ds(first * SUBLANES + c, n, stride=SUBLANES), :] for c in range(SUBLANES)], axis=1)


def _tile_copy(src_ref, src_tile, dst_ref, dst_tile, sem):
    return pltpu.make_async_copy(src_ref.at[pl.ds(pl.multiple_of(src_tile * SUBLANES, SUBLANES), SUBLANES)],
                                 dst_ref.at[pl.ds(pl.multiple_of(dst_tile * SUBLANES, SUBLANES), SUBLANES)], sem)


def _start_tile_gather(src_ref, idx_ref, dst_ref, sem, n_tiles):
    def trip(g, c):
        for u in range(GATHER_UNROLL):
            r = g * GATHER_UNROLL + u
            _tile_copy(src_ref, idx_ref[0, 0, r], dst_ref, r, sem).start()
        return c

    lax.fori_loop(0, n_tiles // GATHER_UNROLL, trip, 0)


def _wait_tile_gather(src_ref, dst_ref, sem, n_tiles):
    def trip(g, c):
        for u in range(GATHER_UNROLL):
            _tile_copy(src_ref, 0, dst_ref, g * GATHER_UNROLL + u, sem).wait()
        return c

    lax.fori_loop(0, n_tiles // GATHER_UNROLL, trip, 0)


def _expert_kernel(blk_exp, n_used, tok_ref, tok_next_ref, h_ref, wg_ref, wu_ref, wd_ref, y_ref,
                   xbuf, xb_ref, acc_ref, sem, *, mb):
    b = pl.program_id(0)
    j = pl.program_id(1)
    active = b < n_used[0]
    last = j == pl.num_programs(1) - 1
    slot = b % 2

    def gather(idx_ref, s):
        _start_tile_gather(h_ref, idx_ref, xbuf.at[s], sem.at[s], mb)

    @pl.when(jnp.logical_and(b == 0, j == 0))
    def _():
        gather(tok_ref, 0)

    @pl.when(jnp.logical_and(active, j == 0))
    def _():
        _wait_tile_gather(h_ref, xbuf.at[slot], sem.at[slot], mb)
        xb_ref[...] = _tiles_to_rows(xbuf.at[slot], 0, mb).astype(BF16)
        acc_ref[...] = jnp.zeros(acc_ref.shape, F32)

    @pl.when(jnp.logical_and(b + 1 < n_used[0], j == 1))
    def _():
        gather(tok_next_ref, 1 - slot)

    @pl.when(active)
    def _():
        x = xb_ref[...]
        a = jax.nn.silu(jnp.dot(x, wg_ref[0], preferred_element_type=F32)) * jnp.dot(
            x, wu_ref[0], preferred_element_type=F32)
        acc_ref[...] += jnp.dot(a.astype(BF16), wd_ref[0], preferred_element_type=F32)

    @pl.when(jnp.logical_and(active, last))
    def _():
        _rows_to_tiles(y_ref, acc_ref[...])

    @pl.when(jnp.logical_and(jnp.logical_not(active), last))
    def _():
        y_ref[...] = jnp.zeros(y_ref.shape, F32)


def expert_ffn(hf, slot_tok, blk_exp, n_used, ex_gate, ex_up, ex_down, mb, tf):
    d = hf.shape[1] * SUBLANES
    n_blk = blk_exp.shape[0]
    ff = ex_gate.shape[2]
    assert ff // tf >= 2
    tok3 = slot_tok.reshape(n_blk, 1, mb)
    grid_spec = pltpu.PrefetchScalarGridSpec(
        num_scalar_prefetch=2,
        grid=(n_blk, ff // tf),
        in_specs=[pl.BlockSpec((1, 1, mb), lambda b, j, be, nu: (b, 0, 0), memory_space=pltpu.SMEM),
                  pl.BlockSpec((1, 1, mb), lambda b, j, be, nu: (jnp.minimum(b + 1, n_blk - 1), 0, 0),
                               memory_space=pltpu.SMEM),
                  pl.BlockSpec(memory_space=pl.ANY),
                  pl.BlockSpec((1, d, tf), lambda b, j, be, nu: (be[b], 0, j)),
                  pl.BlockSpec((1, d, tf), lambda b, j, be, nu: (be[b], 0, j)),
                  pl.BlockSpec((1, tf, d), lambda b, j, be, nu: (be[b], j, 0))],
        out_specs=pl.BlockSpec((mb * SUBLANES, LANES), lambda b, j, be, nu: (b, 0)),
        scratch_shapes=[pltpu.VMEM((2, mb * SUBLANES, LANES), F32), pltpu.VMEM((mb, d), BF16),
                        pltpu.VMEM((mb, d), F32), pltpu.SemaphoreType.DMA((2,))],
    )
    return pl.pallas_call(
        functools.partial(_expert_kernel, mb=mb),
        grid_spec=grid_spec,
        out_shape=jax.ShapeDtypeStruct((n_blk * mb * SUBLANES, LANES), F32),
        compiler_params=_cparams("arbitrary", "arbitrary"),
        name="expert_ffn",
    )(blk_exp, n_used, tok3, tok3, hf, ex_gate, ex_up, ex_down)


def _combine_kernel(pos_ref, pos_next_ref, x_ref, gw_ref, ys_ref, o_ref, buf, sem, *, tm):
    i = pl.program_id(0)
    slot = i % 2
    n_rows = TOP_K * tm

    def gather(idx_ref, s):
        _start_tile_gather(ys_ref, idx_ref, buf.at[s], sem.at[s], n_rows)

    @pl.when(i == 0)
    def _():
        gather(pos_ref, 0)

    @pl.when(i + 1 < pl.num_programs(0))
    def _():
        gather(pos_next_ref, 1 - slot)

    _wait_tile_gather(ys_ref, buf.at[slot], sem.at[slot], n_rows)
    out = x_ref[...]
    gw = gw_ref[...]
    for kk in range(TOP_K):
        out = out + _tiles_to_rows(buf.at[slot], kk * tm, tm) * gw[:, kk:kk + 1]
    o_ref[...] = out


def moe_combine(x2d, gw, ys, pos, tm):
    m, d = x2d.shape
    n_tiles = m // tm
    return pl.pallas_call(
        functools.partial(_combine_kernel, tm=tm),
        grid=(n_tiles,),
        in_specs=[pl.BlockSpec((1, 1, TOP_K * tm), lambda i: (i, 0, 0), memory_space=pltpu.SMEM),
                  pl.BlockSpec((1, 1, TOP_K * tm), lambda i: (jnp.minimum(i + 1, n_tiles - 1), 0, 0),
                               memory_space=pltpu.SMEM),
                  pl.BlockSpec((tm, d), lambda i: (i, 0)),
                  pl.BlockSpec((tm, LANES), lambda i: (i, 0)),
                  pl.BlockSpec(memory_space=pl.ANY)],
        out_specs=pl.BlockSpec((tm, d), lambda i: (i, 0)),
        out_shape=jax.ShapeDtypeStruct((m, d), F32),
        scratch_shapes=[pltpu.VMEM((2, TOP_K * tm * SUBLANES, LANES), F32), pltpu.SemaphoreType.DMA((2,))],
        compiler_params=_cparams("arbitrary"),
        name="moe_combine",
    )(pos, pos, x2d, gw, ys)


def _moe_plan(eid, n_tok, mb):
    n_asg = n_tok * TOP_K
    n_blk = n_asg // mb + N_EXPERTS
    cap = n_blk * mb
    e = eid.reshape(-1)
    tok = jnp.repeat(jnp.arange(n_tok, dtype=I32), TOP_K)
    onehot = (e[:, None] == jnp.arange(N_EXPERTS, dtype=I32)[None, :]).astype(I32)
    rank = jnp.cumsum(onehot, axis=0) - onehot
    rank = jnp.sum(rank * onehot, axis=1)
    counts = jnp.sum(onehot, axis=0)
    padded = (counts + mb - 1) // mb * mb
    pad_end = jnp.cumsum(padded)
    pad_start = pad_end - padded
    dest = (pad_start[e] + rank).astype(I32)
    slot_tok = jnp.zeros((cap,), I32).at[dest].set(tok)
    blk_start = jnp.arange(n_blk, dtype=I32) * mb
    blk_exp = jnp.minimum(jnp.searchsorted(pad_end, blk_start, side='right'), N_EXPERTS - 1).astype(I32)
    n_used = (pad_end[-1] // mb).astype(I32).reshape(1)
    return slot_tok, blk_exp, n_used, dest.reshape(n_tok, TOP_K)


def moe_layer(x2d, g, router_w, ex_gate, ex_up, ex_down, mb=512, tf=896, tm_route=512, tm_comb=256):
    n_tok, d = x2d.shape
    router_pad = jnp.zeros((d, LANES), F32).at[:, :N_EXPERTS].set(router_w)
    hf, eid, gw = router(x2d, g, router_pad, tm_route)
    slot_tok, blk_exp, n_used, dest = _moe_plan(eid[:, :TOP_K], n_tok, mb)
    ys = expert_ffn(hf, slot_tok, blk_exp, n_used, ex_gate, ex_up, ex_down, mb, tf)
    pos = dest.reshape(n_tok // tm_comb, tm_comb, TOP_K).transpose(0, 2, 1).reshape(
        n_tok // tm_comb, 1, TOP_K * tm_comb)
    return moe_combine(x2d, gw, ys, pos, tm_comb)


def _rope_tables(seq):
    def tab(dim):
        inv = 1.0 / (ROPE_THETA ** (jnp.arange(0, dim, 2, dtype=F32) / dim))
        ang = jnp.arange(seq, dtype=F32)[:, None] * inv[None, :]
        cos, sin = jnp.cos(ang), jnp.sin(ang)
        reps = LANES // dim
        return (jnp.tile(jnp.concatenate([cos, cos], axis=1), (1, reps)),
                jnp.tile(jnp.concatenate([-sin, sin], axis=1), (1, reps)))

    ca, sa = tab(HEAD_DIM)
    ci, si = tab(IDX_DIM)
    return ca, sa, ci, si


def _pack_w_in(w):
    d_blk = N_HEADS * HEAD_DIM
    n_idx = N_IDX_HEADS * IDX_DIM + IDX_DIM + N_IDX_HEADS
    v_lo, v_hi = 5 * d_blk, 6 * d_blk
    main = jnp.concatenate([w[:, :v_lo], w[:, v_hi + n_idx:]], axis=1)
    idx = jnp.pad(w[:, v_hi:v_hi + n_idx], ((0, 0), (0, IDX_COLS - n_idx)))
    return main.astype(BF16), w[:, v_lo:v_hi].T.astype(BF16), idx.astype(BF16)


def kernel(x, mem, attn_norm, w_in, conv_w, q_norm, k_norm, mem_norm, w_mem_kv, mq_norm, mk_norm, w_br_conv, w_br_attn, w_br_mem, w_out, ffn_norm, ff_gate, ff_up, ff_down, router, ex_gate, ex_up, ex_down):
    batch, seq, d = x.shape
    depth = attn_norm.shape[0]
    n_mem = mem.shape[1]
    topk = min(MAX_TOPK, seq // 4)
    m = batch * seq
    rope = _rope_tables(seq)
    mem2d = mem.reshape(batch * n_mem, d)
    xc = x.reshape(m, d)
    row = lambda a: a.reshape(1, -1)
    for l in range(depth):
        w_main, w_vt, w_idx = _pack_w_in(w_in[l])
        g = row(attn_norm[l])
        p_main = norm_matmul(xc, g, w_main, tm=1024, tn=1024)
        p_idx = norm_matmul(xc, g, w_idx, tm=1024, tn=IDX_COLS)
        vt = norm_matmul_t(xc, g, w_vt, seq, tm=512)
        qn, kn, qip, kil, wis = dsa_prep(p_main, p_idx, rope, row(q_norm[l]), row(k_norm[l]), seq, ts=512)
        att = dsa_attention(qn, qip, wis, kil, kn, vt, _logits_bounded(q_norm[l], k_norm[l]),
                            batch, seq, topk)
        mk, mv = mem_kv(mem2d, row(mem_norm[l]), w_mem_kv[l].astype(BF16), row(mk_norm[l]), batch, n_mem)
        memo = mem_attention(p_main, mk, mv, row(mq_norm[l]), seq, n_mem, tm=512)
        xc = merge(xc, p_main, att, memo, conv_w[l], w_br_conv[l].astype(BF16), w_br_attn[l].astype(BF16),
                   w_br_mem[l].astype(BF16), w_out[l].astype(BF16), seq, tm=256)
        j = l // 2
        if l % 2 == 0:
            xc = ffn_dense(xc, row(ffn_norm[l]), ff_gate[j].astype(BF16), ff_up[j].astype(BF16),
                           ff_down[j].astype(BF16), tm=512, tf=512)
        else:
            xc = moe_layer(xc, row(ffn_norm[l]), router[j], ex_gate[j].astype(BF16), ex_up[j].astype(BF16),
                           ex_down[j].astype(BF16))
    return xc.reshape(batch, seq, d)
```

```python
import functools
import math

import numpy as np
import jax
import jax.numpy as jnp
from jax import lax
from jax.experimental import pallas as pl
from jax.experimental.pallas import tpu as pltpu

F32 = jnp.float32
BF16 = jnp.bfloat16
I32 = jnp.int32

N_HEADS = 8
HEAD_DIM = 128
N_IDX_HEADS = 8
IDX_DIM = 64
MAX_TOPK = 256
N_MEM_HEADS = 4
MEM_HEAD_DIM = 256
N_EXPERTS = 8
TOP_K = 2
CONV_WIDTH = 3
ROPE_THETA = 10000.0
EPS = 1e-6

LANES = 128
SUBLANES = 8
INT_MIN = -(2 ** 31)
INT_MAX = 2 ** 31 - 1
LOG2E = math.log2(math.e)
VMEM_LIMIT = 56 * 1024 * 1024

COL_CB, COL_CC, COL_CU, COL_Q, COL_K, COL_QM, COL_GC, COL_GA, COL_GM = range(9)
IDX_COLS = 640


def _cparams(*sem):
    return pltpu.CompilerParams(dimension_semantics=sem, vmem_limit_bytes=VMEM_LIMIT)


def _rms(x, g):
    return x * lax.rsqrt(jnp.mean(x * x, axis=-1, keepdims=True) + EPS) * g


def _norm_matmul_kernel(x_ref, g_ref, w_ref, o_ref, h_ref):
    @pl.when(pl.program_id(1) == 0)
    def _():
        h_ref[...] = _rms(x_ref[...], g_ref[...]).astype(BF16)

    o_ref[...] = jnp.dot(h_ref[...], w_ref[...], preferred_element_type=F32)


def norm_matmul(x, g, w, tm, tn):
    m, k = x.shape
    n = w.shape[1]
    return pl.pallas_call(
        _norm_matmul_kernel,
        grid=(m // tm, n // tn),
        in_specs=[pl.BlockSpec((tm, k), lambda i, j: (i, 0)),
                  pl.BlockSpec((1, k), lambda i, j: (0, 0)),
                  pl.BlockSpec((k, tn), lambda i, j: (0, j))],
        out_specs=pl.BlockSpec((tm, tn), lambda i, j: (i, j)),
        out_shape=jax.ShapeDtypeStruct((m, n), F32),
        scratch_shapes=[pltpu.VMEM((tm, k), BF16)],
        compiler_params=_cparams("parallel", "arbitrary"),
        name="norm_matmul",
    )(x, g, w)


def _norm_matmul_t_kernel(x_ref, g_ref, wt_ref, o_ref):
    h = _rms(x_ref[...], g_ref[...]).astype(BF16)
    o_ref[...] = lax.dot_general(wt_ref[...], h, (((1,), (1,)), ((), ())),
                                 preferred_element_type=F32).astype(o_ref.dtype)


def norm_matmul_t(x, g, wt, seq, tm):
    m, k = x.shape
    n = wt.shape[0]
    tiles_per_seq = seq // tm
    return pl.pallas_call(
        _norm_matmul_t_kernel,
        grid=(m // tm,),
        in_specs=[pl.BlockSpec((tm, k), lambda i: (i, 0)),
                  pl.BlockSpec((1, k), lambda i: (0, 0)),
                  pl.BlockSpec((n, k), lambda i: (0, 0))],
        out_specs=pl.BlockSpec((n, tm), lambda i: (i // tiles_per_seq, i % tiles_per_seq)),
        out_shape=jax.ShapeDtypeStruct((m // seq * n, seq), BF16),
        compiler_params=_cparams("parallel"),
        name="norm_matmul_t",
    )(x, g, wt)


def _prep_kernel(q_ref, k_ref, idx_ref, ca_ref, sa_ref, ci_ref, si_ref, qg_ref, kg_ref,
                 qn_ref, kn_ref, qip_ref, kil_ref, wis_ref, *, q_scale, w_scale):
    ca = ca_ref[...]
    sa = sa_ref[...]
    for h in range(N_HEADS):
        sl = slice(h * HEAD_DIM, (h + 1) * HEAD_DIM)
        for src, g_ref, dst, scale in ((q_ref, qg_ref, qn_ref, q_scale), (k_ref, kg_ref, kn_ref, 1.0)):
            n = _rms(src[:, sl], g_ref[...])
            o = n * ca + pltpu.roll(n, HEAD_DIM // 2, 1) * sa
            dst[:, sl] = (o * scale).astype(BF16)

    ci = ci_ref[...]
    si = si_ref[...]
    lane = lax.broadcasted_iota(I32, ci.shape, 1)
    first_half = (lane & (IDX_DIM // 2)) == 0
    low_head = lane < IDX_DIM

    def idx_rope(x):
        rot = jnp.where(first_half, pltpu.roll(x, LANES - IDX_DIM // 2, 1), pltpu.roll(x, IDX_DIM // 2, 1))
        return x * ci + rot * si

    for c in range(N_IDX_HEADS // 2):
        o = idx_rope(idx_ref[:, c * LANES:(c + 1) * LANES])
        qip_ref[:, (2 * c) * LANES:(2 * c + 1) * LANES] = jnp.where(low_head, o, 0.0).astype(BF16)
        qip_ref[:, (2 * c + 1) * LANES:(2 * c + 2) * LANES] = jnp.where(
            low_head, pltpu.roll(o, IDX_DIM, 1), 0.0).astype(BF16)
    tail = idx_ref[:, N_IDX_HEADS * IDX_DIM:N_IDX_HEADS * IDX_DIM + LANES]
    kil_ref[...] = jnp.where(low_head, idx_rope(tail), 0.0).astype(BF16)
    wis_ref[...] = tail * w_scale


def dsa_prep(p_main, p_idx, rope, q_norm, k_norm, seq, ts):
    m = p_main.shape[0]
    d = N_HEADS * HEAD_DIM
    n_seq_tiles = seq // ts
    ca, sa, ci, si = rope
    row = lambda c: pl.BlockSpec((ts, d), lambda i, c=c: (i, c))
    tab = pl.BlockSpec((ts, LANES), lambda i: (i % n_seq_tiles, 0))
    gain = pl.BlockSpec((1, HEAD_DIM), lambda i: (0, 0))
    out_d = pl.BlockSpec((ts, d), lambda i: (i, 0))
    out_l = pl.BlockSpec((ts, LANES), lambda i: (i, 0))
    q_scale = HEAD_DIM ** -0.5 * LOG2E
    w_scale = (N_IDX_HEADS * IDX_DIM) ** -0.5
    return pl.pallas_call(
        functools.partial(_prep_kernel, q_scale=q_scale, w_scale=w_scale),
        grid=(m // ts,),
        in_specs=[row(COL_Q), row(COL_K),
                  pl.BlockSpec((ts, IDX_COLS), lambda i: (i, 0)),
                  tab, tab, tab, tab, gain, gain],
        out_specs=[out_d, out_d, out_d, out_l, out_l],
        out_shape=[jax.ShapeDtypeStruct((m, d), BF16)] * 3
        + [jax.ShapeDtypeStruct((m, LANES), BF16), jax.ShapeDtypeStruct((m, LANES), F32)],
        compiler_params=_cparams("parallel"),
        name="dsa_prep",
    )(p_main, p_main, p_idx, ca, sa, ci, si, q_norm, k_norm)


SCORE_ROWS = 512
SLAB = 32
NT_DIMS = (((1,), (1,)), ((), ()))
ZERO_SCORE_BASE = 2 ** 23 - 1


def _dsa_kernel(qi_of, kb_of, last_of, nsc_of, ncover_of, bounded,
                qn_ref, qip_ref, wis_ref, kil_ref, kn_ref, vt_ref,
                o_ref,
                keys_ref, thr_ref, w_ref, m_ref, l_ref, acc_ref, s_ref, gmax_ref,
                *, tq, tk, topk):
    p = pl.program_id(1)
    qi = qi_of[p]
    kb = kb_of[p]

    @pl.when(kb == 0)
    def _():
        w_ref[...] = wis_ref[...].T[IDX_DIM:IDX_DIM + N_IDX_HEADS, :]
        q_id = qi * tq + lax.broadcasted_iota(I32, (SCORE_ROWS, tq), 1)
        k_iota = lax.broadcasted_iota(I32, (SCORE_ROWS, tq), 0)
        n_sc = nsc_of[p]

        def score_step(c, carry):
            start = pl.multiple_of(c * SCORE_ROWS, SCORE_ROWS)
            kblk = kil_ref[pl.ds(start, SCORE_ROWS), :]
            sc = jnp.zeros((SCORE_ROWS, tq), F32)
            for h in range(N_IDX_HEADS):
                z = lax.dot_general(kblk, qip_ref[:, h * LANES:(h + 1) * LANES], NT_DIMS,
                                    preferred_element_type=F32)
                sc = sc + jnp.maximum(z, 0.0) * w_ref[h:h + 1, :]
            bits = pltpu.bitcast(sc, I32)
            key = jnp.where(bits < 0, bits ^ 0x7FFFFFFF, bits)
            k_id = k_iota + start
            key = jnp.where(key == 0, ZERO_SCORE_BASE - k_id, key)
            key = jnp.where(k_id <= q_id, key, INT_MIN)
            keys_ref[pl.ds(start, SCORE_ROWS), :] = key
            gmax = gmax_ref[...]
            for j in range(SCORE_ROWS // MAX_TOPK):
                gmax = jnp.maximum(gmax, key[j * MAX_TOPK:(j + 1) * MAX_TOPK, :])
            gmax_ref[...] = gmax
            return carry

        gmax_ref[...] = jnp.full(gmax_ref.shape, INT_MIN, I32)
        lax.fori_loop(0, n_sc, score_step, 0)

        def fill_step(c, carry):
            start = pl.multiple_of(c * SCORE_ROWS, SCORE_ROWS)
            keys_ref[pl.ds(start, SCORE_ROWS), :] = jnp.full((SCORE_ROWS, tq), INT_MIN, I32)
            return carry

        lax.fori_loop(n_sc, ncover_of[p], fill_step, 0)

        def count_ge(cand):
            def count_step(c, acc):
                start = pl.multiple_of(c * SCORE_ROWS, SCORE_ROWS)
                for j in range(SCORE_ROWS // SLAB):
                    acc = acc + (keys_ref[pl.ds(start + j * SLAB, SLAB), :] >= cand).astype(I32)
                return acc

            acc = lax.fori_loop(0, n_sc, count_step, jnp.zeros((SLAB, tq), I32))
            return jnp.sum(acc.astype(F32), axis=0, keepdims=True)

        gmax = gmax_ref[...]
        lo0 = jnp.maximum(jnp.min(gmax, axis=0, keepdims=True), INT_MIN + 1)
        hi0 = jnp.minimum(jnp.max(gmax, axis=0, keepdims=True), INT_MAX - 1) + 1

        def open_interval(lo, hi):
            return jnp.max(((lo + 1) < hi).astype(I32)) > 0

        def halve(lo, hi):
            mid = (lo & hi) + ((lo ^ hi) >> 1)
            cnt = count_ge(mid)
            keep = cnt >= topk
            lo = jnp.where(keep, mid, lo)
            hi = jnp.where(cnt == topk, mid + 1, jnp.where(keep, hi, mid))
            return lo, hi

        def bisect(state):
            lo, hi = halve(*halve(state[0], state[1]))
            return lo, hi, open_interval(lo, hi)

        lo, _, _ = lax.while_loop(lambda st: st[2], bisect, (lo0, hi0, open_interval(lo0, hi0)))
        thr_ref[...] = lo

        m_ref[...] = jnp.full(m_ref.shape, -jnp.inf, F32)
        l_ref[...] = jnp.zeros(l_ref.shape, F32)
        acc_ref[...] = jnp.zeros(acc_ref.shape, F32)

    def mask_bias():
        keys = keys_ref[pl.ds(pl.multiple_of(kb * tk, tk), tk), :]
        return jnp.where(keys >= thr_ref[...], 0.0, -jnp.inf).astype(F32)

    def masked_logits(h, bias):
        sl = slice(h * HEAD_DIM, (h + 1) * HEAD_DIM)
        return lax.dot_general(kn_ref[:, sl], qn_ref[:, sl], NT_DIMS, preferred_element_type=F32) + bias


    @pl.when(bounded[0] == 1)
    def _():
        bias = mask_bias()
        s_ref[0] = masked_logits(0, bias)
        for h in range(N_HEADS):
            sl = slice(h * HEAD_DIM, (h + 1) * HEAD_DIM)
            if h + 1 < N_HEADS:
                s_ref[(h + 1) % 2] = masked_logits(h + 1, bias)
            pr = jnp.exp2(s_ref[h % 2])
            l_ref[h:h + 1, :] = l_ref[h:h + 1, :] + jnp.sum(pr, axis=0, keepdims=True)
            acc_ref[sl, :] = acc_ref[sl, :] + jnp.dot(
                vt_ref[sl, :], pr.astype(BF16), preferred_element_type=F32)

    @pl.when(bounded[0] == 0)
    def _():
        bias = mask_bias()

        def logits(h):
            s = masked_logits(h, bias)
            s_ref[h % 2] = s
            return jnp.max(s, axis=0, keepdims=True)

        def accumulate(h, m_cur):
            sl = slice(h * HEAD_DIM, (h + 1) * HEAD_DIM)
            m_prev = m_ref[h:h + 1, :]
            m_new = jnp.maximum(m_prev, m_cur)
            m_safe = jnp.where(m_new == -jnp.inf, 0.0, m_new)
            alpha = jnp.exp2(m_prev - m_safe)
            pr = jnp.exp2(s_ref[h % 2] - m_safe)
            l_ref[h:h + 1, :] = alpha * l_ref[h:h + 1, :] + jnp.sum(pr, axis=0, keepdims=True)
            m_ref[h:h + 1, :] = m_new
            acc_ref[sl, :] = acc_ref[sl, :] * alpha + jnp.dot(
                vt_ref[sl, :], pr.astype(BF16), preferred_element_type=F32)

        m_cur = logits(0)
        for h in range(N_HEADS):
            m_next = logits(h + 1) if h + 1 < N_HEADS else None
            accumulate(h, m_cur)
            m_cur = m_next

    @pl.when(last_of[p] == 1)
    def _():
        for h in range(N_HEADS):
            sl = slice(h * HEAD_DIM, (h + 1) * HEAD_DIM)
            o_ref[:, sl] = (acc_ref[sl, :] / l_ref[h:h + 1, :]).T.astype(o_ref.dtype)


def _dsa_schedule(seq, tq, tk):
    assert tk % SCORE_ROWS == 0
    qi_of, kb_of, last_of, nsc_of, ncover_of = [], [], [], [], []
    for qi in range(seq // tq):
        last_kb = ((qi + 1) * tq - 1) // tk
        for kb in range(last_kb + 1):
            qi_of.append(qi)
            kb_of.append(kb)
            last_of.append(int(kb == last_kb))
            nsc_of.append(-(-((qi + 1) * tq) // SCORE_ROWS))
            ncover_of.append((last_kb + 1) * (tk // SCORE_ROWS))
    return tuple(np.asarray(a, np.int32) for a in (qi_of, kb_of, last_of, nsc_of, ncover_of))


def dsa_attention(qn, qip, wis, kil, kn, vt, bounded, batch, seq, topk, tq=256, tk=1024):
    d = N_HEADS * HEAD_DIM
    nq = seq // tq
    nk = seq // tk
    assert seq <= ZERO_SCORE_BASE and topk <= MAX_TOPK
    sched = _dsa_schedule(seq, tq, tk)
    n_steps = len(sched[0])
    qrow = lambda w: pl.BlockSpec((tq, w), lambda b, p, qi_of, kb_of, *_: (b * nq + qi_of[p], 0))
    grid_spec = pltpu.PrefetchScalarGridSpec(
        num_scalar_prefetch=6,
        grid=(batch, n_steps),
        in_specs=[qrow(d), qrow(d), qrow(LANES),
                  pl.BlockSpec((seq, LANES), lambda b, p, *_: (b, 0)),
                  pl.BlockSpec((tk, d), lambda b, p, qi_of, kb_of, *_: (b * nk + kb_of[p], 0)),
                  pl.BlockSpec((d, tk), lambda b, p, qi_of, kb_of, *_: (b, kb_of[p]))],
        out_specs=qrow(d),
        scratch_shapes=[pltpu.VMEM((seq, tq), I32),
                        pltpu.VMEM((1, tq), I32),
                        pltpu.VMEM((N_IDX_HEADS, tq), F32),
                        pltpu.VMEM((N_HEADS, tq), F32),
                        pltpu.VMEM((N_HEADS, tq), F32),
                        pltpu.VMEM((d, tq), F32),
                        pltpu.VMEM((2, tk, tq), F32),
                        pltpu.VMEM((MAX_TOPK, tq), I32)],
    )
    return pl.pallas_call(
        functools.partial(_dsa_kernel, tq=tq, tk=tk, topk=topk),
        grid_spec=grid_spec,
        out_shape=jax.ShapeDtypeStruct((batch * seq, d), BF16),
        compiler_params=_cparams("parallel", "arbitrary"),
        name="dsa_attention",
    )(*sched, bounded, qn, qip, wis, kil, kn, vt)


LOGIT_BOUND = 100.0


def _logits_bounded(q_gain, k_gain):
    bound = HEAD_DIM * jnp.max(jnp.abs(q_gain)) * jnp.max(jnp.abs(k_gain)) * (HEAD_DIM ** -0.5 * LOG2E)
    return (bound <= LOGIT_BOUND).astype(I32).reshape(1)


def _mem_kv_kernel(mem_ref, g_ref, w_ref, kg_ref, mk_ref, mv_ref):
    d = N_MEM_HEADS * MEM_HEAD_DIM
    h = _rms(mem_ref[...], g_ref[...]).astype(BF16)
    kv = jnp.dot(h, w_ref[...], preferred_element_type=F32)
    for hd in range(N_MEM_HEADS):
        sl = slice(hd * MEM_HEAD_DIM, (hd + 1) * MEM_HEAD_DIM)
        mk_ref[:, sl] = _rms(kv[:, sl], kg_ref[...]).astype(BF16)
    mv_ref[...] = kv[:, d:].astype(BF16)


def mem_kv(mem2d, mem_norm, w_mem_kv, mk_norm, batch, n_mem):
    dm = mem2d.shape[1]
    d = N_MEM_HEADS * MEM_HEAD_DIM
    blk = pl.BlockSpec((n_mem, d), lambda b: (b, 0))
    return pl.pallas_call(
        _mem_kv_kernel,
        grid=(batch,),
        in_specs=[pl.BlockSpec((n_mem, dm), lambda b: (b, 0)),
                  pl.BlockSpec((1, dm), lambda b: (0, 0)),
                  pl.BlockSpec((dm, 2 * d), lambda b: (0, 0)),
                  pl.BlockSpec((1, MEM_HEAD_DIM), lambda b: (0, 0))],
        out_specs=[blk, blk],
        out_shape=[jax.ShapeDtypeStruct((batch * n_mem, d), BF16)] * 2,
        compiler_params=_cparams("arbitrary"),
        name="mem_kv",
    )(mem2d, mem_norm, w_mem_kv, mk_norm)


def _mem_attn_kernel(q_ref, mk_ref, mv_ref, g_ref, o_ref, *, scale):
    for hd in range(N_MEM_HEADS):
        sl = slice(hd * MEM_HEAD_DIM, (hd + 1) * MEM_HEAD_DIM)
        qh = (_rms(q_ref[:, sl], g_ref[...]) * scale).astype(BF16)
        s = lax.dot_general(qh, mk_ref[:, sl], (((1,), (1,)), ((), ())), preferred_element_type=F32)
        pr = jnp.exp2(s - jnp.max(s, axis=1, keepdims=True))
        o = jnp.dot(pr.astype(BF16), mv_ref[:, sl], preferred_element_type=F32)
        o_ref[:, sl] = (o / jnp.sum(pr, axis=1, keepdims=True)).astype(o_ref.dtype)


def mem_attention(p_main, mk, mv, mq_norm, seq, n_mem, tm):
    m = p_main.shape[0]
    d = N_MEM_HEADS * MEM_HEAD_DIM
    tiles_per_seq = seq // tm
    kv = pl.BlockSpec((n_mem, d), lambda i: (i // tiles_per_seq, 0))
    return pl.pallas_call(
        functools.partial(_mem_attn_kernel, scale=MEM_HEAD_DIM ** -0.5 * LOG2E),
        grid=(m // tm,),
        in_specs=[pl.BlockSpec((tm, d), lambda i: (i, COL_QM)), kv, kv,
                  pl.BlockSpec((1, MEM_HEAD_DIM), lambda i: (0, 0))],
        out_specs=pl.BlockSpec((tm, d), lambda i: (i, 0)),
        out_shape=jax.ShapeDtypeStruct((m, d), BF16),
        compiler_params=_cparams("parallel"),
        name="mem_attention",
    )(p_main, mk, mv, mq_norm)


HALO = 8


def _merge_kernel(x_ref, cb_ref, cc_ref, cu_ref, cch_ref, cuh_ref, gc_ref, ga_ref, gm_ref,
                  att_ref, mem_ref, cw_ref, wc_ref, wa_ref, wm_ref, wo_ref, o_ref, *, tiles_per_seq):
    i = pl.program_id(0)
    pcur = cc_ref[...] * cu_ref[...]
    halo = jnp.where(i % tiles_per_seq == 0, 0.0, cch_ref[...] * cuh_ref[...])
    row = lax.broadcasted_iota(I32, pcur.shape, 0)
    h1 = halo[HALO - 1:HALO, :]
    h2 = halo[HALO - 2:HALO - 1, :]
    p1 = jnp.where(row == 0, h1, pltpu.roll(pcur, 1, 0))
    p2 = jnp.where(row == 0, h2, jnp.where(row == 1, h1, pltpu.roll(pcur, 2, 0)))
    cw = cw_ref[...]
    conv = cb_ref[...] * (cw[0:1, :] * p2 + cw[1:2, :] * p1 + cw[2:3, :] * pcur)
    merged = jax.nn.sigmoid(gc_ref[...]) * jnp.dot(conv.astype(BF16), wc_ref[...], preferred_element_type=F32)
    merged += jax.nn.sigmoid(ga_ref[...]) * jnp.dot(att_ref[...], wa_ref[...], preferred_element_type=F32)
    merged += jax.nn.sigmoid(gm_ref[...]) * jnp.dot(mem_ref[...], wm_ref[...], preferred_element_type=F32)
    o_ref[...] = x_ref[...] + jnp.dot(merged.astype(BF16), wo_ref[...], preferred_element_type=F32)


def merge(x2d, p_main, att, memo, conv_w, w_c, w_a, w_m, w_o, seq, tm):
    m, d = x2d.shape
    tiles_per_seq = seq // tm
    col = lambda c: pl.BlockSpec((tm, d), lambda i, c=c: (i, c))
    halo = lambda c: pl.BlockSpec((HALO, d), lambda i, c=c: (jnp.maximum(i * (tm // HALO) - 1, 0), c))
    rowb = pl.BlockSpec((tm, d), lambda i: (i, 0))
    wspec = pl.BlockSpec((d, d), lambda i: (0, 0))
    return pl.pallas_call(
        functools.partial(_merge_kernel, tiles_per_seq=tiles_per_seq),
        grid=(m // tm,),
        in_specs=[rowb, col(COL_CB), col(COL_CC), col(COL_CU), halo(COL_CC), halo(COL_CU),
                  col(COL_GC), col(COL_GA), col(COL_GM), rowb, rowb,
                  pl.BlockSpec((CONV_WIDTH, d), lambda i: (0, 0)),
                  wspec, wspec, wspec, wspec],
        out_specs=rowb,
        out_shape=jax.ShapeDtypeStruct((m, d), F32),
        compiler_params=_cparams("parallel"),
        name="merge",
    )(x2d, p_main, p_main, p_main, p_main, p_main, p_main, p_main, p_main, att, memo,
      conv_w, w_c, w_a, w_m, w_o)


def _ffn_kernel(x_ref, g_ref, wg_ref, wu_ref, wd_ref, o_ref, h_ref, acc_ref):
    j = pl.program_id(1)

    @pl.when(j == 0)
    def _():
        h_ref[...] = _rms(x_ref[...], g_ref[...]).astype(BF16)
        acc_ref[...] = jnp.zeros(acc_ref.shape, F32)

    h = h_ref[...]
    a = jax.nn.silu(jnp.dot(h, wg_ref[...], preferred_element_type=F32)) * jnp.dot(
        h, wu_ref[...], preferred_element_type=F32)
    acc_ref[...] += jnp.dot(a.astype(BF16), wd_ref[...], preferred_element_type=F32)

    @pl.when(j == pl.num_programs(1) - 1)
    def _():
        o_ref[...] = x_ref[...] + acc_ref[...]


def ffn_dense(x2d, g, w_gate, w_up, w_down, tm, tf):
    m, d = x2d.shape
    ff = w_gate.shape[1]
    return pl.pallas_call(
        _ffn_kernel,
        grid=(m // tm, ff // tf),
        in_specs=[pl.BlockSpec((tm, d), lambda i, j: (i, 0)),
                  pl.BlockSpec((1, d), lambda i, j: (0, 0)),
                  pl.BlockSpec((d, tf), lambda i, j: (0, j)),
                  pl.BlockSpec((d, tf), lambda i, j: (0, j)),
                  pl.BlockSpec((tf, d), lambda i, j: (j, 0))],
        out_specs=pl.BlockSpec((tm, d), lambda i, j: (i, 0)),
        out_shape=jax.ShapeDtypeStruct((m, d), F32),
        scratch_shapes=[pltpu.VMEM((tm, d), BF16), pltpu.VMEM((tm, d), F32)],
        compiler_params=_cparams("parallel", "arbitrary"),
        name="ffn_dense",
    )(x2d, g, w_gate, w_up, w_down)


def _router_kernel(x_ref, g_ref, r_ref, h_ref, eid_ref, gw_ref):
    h = _rms(x_ref[...], g_ref[...])
    _rows_to_tiles(h_ref, h)
    logits = jnp.dot(h, r_ref[...], preferred_element_type=F32, precision=lax.Precision.HIGHEST)
    lane = lax.broadcasted_iota(I32, logits.shape, 1)
    lanef = lane.astype(F32)
    logits = jnp.where(lane < N_EXPERTS, logits, -jnp.inf)
    m1 = jnp.max(logits, axis=1, keepdims=True)
    i1 = jnp.min(jnp.where(logits == m1, lanef, float(LANES)), axis=1, keepdims=True)
    rest = jnp.where(lanef == i1, -jnp.inf, logits)
    m2 = jnp.max(rest, axis=1, keepdims=True)
    i2 = jnp.min(jnp.where(rest == m2, lanef, float(LANES)), axis=1, keepdims=True)
    e = jnp.exp(m2 - m1)
    g1 = 1.0 / (1.0 + e)
    eid_ref[...] = jnp.where(lane == 0, i1, jnp.where(lane == 1, i2, 0.0)).astype(I32)
    gw_ref[...] = jnp.where(lane == 0, g1, jnp.where(lane == 1, e * g1, 0.0))


def router(x2d, g, router_pad, tm):
    m, d = x2d.shape
    assert d == SUBLANES * LANES
    lane_out = pl.BlockSpec((tm, LANES), lambda i: (i, 0))
    return pl.pallas_call(
        _router_kernel,
        grid=(m // tm,),
        in_specs=[pl.BlockSpec((tm, d), lambda i: (i, 0)),
                  pl.BlockSpec((1, d), lambda i: (0, 0)),
                  pl.BlockSpec((d, LANES), lambda i: (0, 0))],
        out_specs=[pl.BlockSpec((tm * SUBLANES, LANES), lambda i: (i, 0)), lane_out, lane_out],
        out_shape=[jax.ShapeDtypeStruct((m * SUBLANES, LANES), F32),
                   jax.ShapeDtypeStruct((m, LANES), I32),
                   jax.ShapeDtypeStruct((m, LANES), F32)],
        compiler_params=_cparams("parallel"),
        name="router",
    )(x2d, g, router_pad)


GATHER_UNROLL = 8


def _rows_to_tiles(ref, val):
    n = val.shape[0]
    for c in range(SUBLANES):
        ref[pl.ds(c, n, stride=SUBLANES), :] = val[:, c * LANES:(c + 1) * LANES]


def _tiles_to_rows(ref, first, n):
    return jnp.concatenate(
        [ref[pl.ds(first * SUBLANES + c, n, stride=SUBLANES), :] for c in range(SUBLANES)], axis=1)


def _tile_copy(src_ref, src_tile, dst_ref, dst_tile, sem):
    return pltpu.make_async_copy(src_ref.at[pl.ds(pl.multiple_of(src_tile * SUBLANES, SUBLANES), SUBLANES)],
                                 dst_ref.at[pl.ds(pl.multiple_of(dst_tile * SUBLANES, SUBLANES), SUBLANES)], sem)


def _start_tile_gather(src_ref, idx_ref, dst_ref, sem, n_tiles):
    def trip(g, c):
        for u in range(GATHER_UNROLL):
            r = g * GATHER_UNROLL + u
            _tile_copy(src_ref, idx_ref[0, 0, r], dst_ref, r, sem).start()
        return c

    lax.fori_loop(0, n_tiles // GATHER_UNROLL, trip, 0)


def _wait_tile_gather(src_ref, dst_ref, sem, n_tiles):
    def trip(g, c):
        for u in range(GATHER_UNROLL):
            _tile_copy(src_ref, 0, dst_ref, g * GATHER_UNROLL + u, sem).wait()
        return c

    lax.fori_loop(0, n_tiles // GATHER_UNROLL, trip, 0)


def _expert_kernel(blk_exp, n_used, tok_ref, tok_next_ref, h_ref, wg_ref, wu_ref, wd_ref, y_ref,
                   xbuf, xb_ref, acc_ref, sem, *, mb):
    b = pl.program_id(0)
    j = pl.program_id(1)
    active = b < n_used[0]
    last = j == pl.num_programs(1) - 1
    slot = b % 2

    def gather(idx_ref, s):
        _start_tile_gather(h_ref, idx_ref, xbuf.at[s], sem.at[s], mb)

    @pl.when(jnp.logical_and(b == 0, j == 0))
    def _():
        gather(tok_ref, 0)

    @pl.when(jnp.logical_and(active, j == 0))
    def _():
        _wait_tile_gather(h_ref, xbuf.at[slot], sem.at[slot], mb)
        xb_ref[...] = _tiles_to_rows(xbuf.at[slot], 0, mb).astype(BF16)
        acc_ref[...] = jnp.zeros(acc_ref.shape, F32)

    @pl.when(jnp.logical_and(b + 1 < n_used[0], j == 1))
    def _():
        gather(tok_next_ref, 1 - slot)

    @pl.when(active)
    def _():
        x = xb_ref[...]
        a = jax.nn.silu(jnp.dot(x, wg_ref[0], preferred_element_type=F32)) * jnp.dot(
            x, wu_ref[0], preferred_element_type=F32)
        acc_ref[...] += jnp.dot(a.astype(BF16), wd_ref[0], preferred_element_type=F32)

    @pl.when(jnp.logical_and(active, last))
    def _():
        _rows_to_tiles(y_ref, acc_ref[...])

    @pl.when(jnp.logical_and(jnp.logical_not(active), last))
    def _():
        y_ref[...] = jnp.zeros(y_ref.shape, F32)


def expert_ffn(hf, slot_tok, blk_exp, n_used, ex_gate, ex_up, ex_down, mb, tf):
    d = hf.shape[1] * SUBLANES
    n_blk = blk_exp.shape[0]
    ff = ex_gate.shape[2]
    assert ff // tf >= 2
    tok3 = slot_tok.reshape(n_blk, 1, mb)
    grid_spec = pltpu.PrefetchScalarGridSpec(
        num_scalar_prefetch=2,
        grid=(n_blk, ff // tf),
        in_specs=[pl.BlockSpec((1, 1, mb), lambda b, j, be, nu: (b, 0, 0), memory_space=pltpu.SMEM),
                  pl.BlockSpec((1, 1, mb), lambda b, j, be, nu: (jnp.minimum(b + 1, n_blk - 1), 0, 0),
                               memory_space=pltpu.SMEM),
                  pl.BlockSpec(memory_space=pl.ANY),
                  pl.BlockSpec((1, d, tf), lambda b, j, be, nu: (be[b], 0, j)),
                  pl.BlockSpec((1, d, tf), lambda b, j, be, nu: (be[b], 0, j)),
                  pl.BlockSpec((1, tf, d), lambda b, j, be, nu: (be[b], j, 0))],
        out_specs=pl.BlockSpec((mb * SUBLANES, LANES), lambda b, j, be, nu: (b, 0)),
        scratch_shapes=[pltpu.VMEM((2, mb * SUBLANES, LANES), F32), pltpu.VMEM((mb, d), BF16),
                        pltpu.VMEM((mb, d), F32), pltpu.SemaphoreType.DMA((2,))],
    )
    return pl.pallas_call(
        functools.partial(_expert_kernel, mb=mb),
        grid_spec=grid_spec,
        out_shape=jax.ShapeDtypeStruct((n_blk * mb * SUBLANES, LANES), F32),
        compiler_params=_cparams("arbitrary", "arbitrary"),
        name="expert_ffn",
    )(blk_exp, n_used, tok3, tok3, hf, ex_gate, ex_up, ex_down)


def _combine_kernel(pos_ref, pos_next_ref, x_ref, gw_ref, ys_ref, o_ref, buf, sem, *, tm):
    i = pl.program_id(0)
    slot = i % 2
    n_rows = TOP_K * tm

    def gather(idx_ref, s):
        _start_tile_gather(ys_ref, idx_ref, buf.at[s], sem.at[s], n_rows)

    @pl.when(i == 0)
    def _():
        gather(pos_ref, 0)

    @pl.when(i + 1 < pl.num_programs(0))
    def _():
        gather(pos_next_ref, 1 - slot)

    _wait_tile_gather(ys_ref, buf.at[slot], sem.at[slot], n_rows)
    out = x_ref[...]
    gw = gw_ref[...]
    for kk in range(TOP_K):
        out = out + _tiles_to_rows(buf.at[slot], kk * tm, tm) * gw[:, kk:kk + 1]
    o_ref[...] = out


def moe_combine(x2d, gw, ys, pos, tm):
    m, d = x2d.shape
    n_tiles = m // tm
    return pl.pallas_call(
        functools.partial(_combine_kernel, tm=tm),
        grid=(n_tiles,),
        in_specs=[pl.BlockSpec((1, 1, TOP_K * tm), lambda i: (i, 0, 0), memory_space=pltpu.SMEM),
                  pl.BlockSpec((1, 1, TOP_K * tm), lambda i: (jnp.minimum(i + 1, n_tiles - 1), 0, 0),
                               memory_space=pltpu.SMEM),
                  pl.BlockSpec((tm, d), lambda i: (i, 0)),
                  pl.BlockSpec((tm, LANES), lambda i: (i, 0)),
                  pl.BlockSpec(memory_space=pl.ANY)],
        out_specs=pl.BlockSpec((tm, d), lambda i: (i, 0)),
        out_shape=jax.ShapeDtypeStruct((m, d), F32),
        scratch_shapes=[pltpu.VMEM((2, TOP_K * tm * SUBLANES, LANES), F32), pltpu.SemaphoreType.DMA((2,))],
        compiler_params=_cparams("arbitrary"),
        name="moe_combine",
    )(pos, pos, x2d, gw, ys)


def _moe_plan(eid, n_tok, mb):
    n_asg = n_tok * TOP_K
    n_blk = n_asg // mb + N_EXPERTS
    cap = n_blk * mb
    e = eid.reshape(-1)
    tok = jnp.repeat(jnp.arange(n_tok, dtype=I32), TOP_K)
    onehot = (e[:, None] == jnp.arange(N_EXPERTS, dtype=I32)[None, :]).astype(I32)
    rank = jnp.cumsum(onehot, axis=0) - onehot
    rank = jnp.sum(rank * onehot, axis=1)
    counts = jnp.sum(onehot, axis=0)
    padded = (counts + mb - 1) // mb * mb
    pad_end = jnp.cumsum(padded)
    pad_start = pad_end - padded
    dest = (pad_start[e] + rank).astype(I32)
    slot_tok = jnp.zeros((cap,), I32).at[dest].set(tok)
    blk_start = jnp.arange(n_blk, dtype=I32) * mb
    blk_exp = jnp.minimum(jnp.searchsorted(pad_end, blk_start, side='right'), N_EXPERTS - 1).astype(I32)
    n_used = (pad_end[-1] // mb).astype(I32).reshape(1)
    return slot_tok, blk_exp, n_used, dest.reshape(n_tok, TOP_K)


def moe_layer(x2d, g, router_w, ex_gate, ex_up, ex_down, mb=512, tf=896, tm_route=512, tm_comb=256):
    n_tok, d = x2d.shape
    router_pad = jnp.zeros((d, LANES), F32).at[:, :N_EXPERTS].set(router_w)
    hf, eid, gw = router(x2d, g, router_pad, tm_route)
    slot_tok, blk_exp, n_used, dest = _moe_plan(eid[:, :TOP_K], n_tok, mb)
    ys = expert_ffn(hf, slot_tok, blk_exp, n_used, ex_gate, ex_up, ex_down, mb, tf)
    pos = dest.reshape(n_tok // tm_comb, tm_comb, TOP_K).transpose(0, 2, 1).reshape(
        n_tok // tm_comb, 1, TOP_K * tm_comb)
    return moe_combine(x2d, gw, ys, pos, tm_comb)


def _rope_tables(seq):
    def tab(dim):
        inv = 1.0 / (ROPE_THETA ** (jnp.arange(0, dim, 2, dtype=F32) / dim))
        ang = jnp.arange(seq, dtype=F32)[:, None] * inv[None, :]
        cos, sin = jnp.cos(ang), jnp.sin(ang)
        reps = LANES // dim
        return (jnp.tile(jnp.concatenate([cos, cos], axis=1), (1, reps)),
                jnp.tile(jnp.concatenate([-sin, sin], axis=1), (1, reps)))

    ca, sa = tab(HEAD_DIM)
    ci, si = tab(IDX_DIM)
    return ca, sa, ci, si


def _pack_w_in(w):
    d_blk = N_HEADS * HEAD_DIM
    n_idx = N_IDX_HEADS * IDX_DIM + IDX_DIM + N_IDX_HEADS
    v_lo, v_hi = 5 * d_blk, 6 * d_blk
    main = jnp.concatenate([w[:, :v_lo], w[:, v_hi + n_idx:]], axis=1)
    idx = jnp.pad(w[:, v_hi:v_hi + n_idx], ((0, 0), (0, IDX_COLS - n_idx)))
    return main.astype(BF16), w[:, v_lo:v_hi].T.astype(BF16), idx.astype(BF16)


def kernel(x, mem, attn_norm, w_in, conv_w, q_norm, k_norm, mem_norm, w_mem_kv, mq_norm, mk_norm, w_br_conv, w_br_attn, w_br_mem, w_out, ffn_norm, ff_gate, ff_up, ff_down, router, ex_gate, ex_up, ex_down):
    batch, seq, d = x.shape
    depth = attn_norm.shape[0]
    n_mem = mem.shape[1]
    topk = min(MAX_TOPK, seq // 4)
    m = batch * seq
    rope = _rope_tables(seq)
    mem2d = mem.reshape(batch * n_mem, d)
    xc = x.reshape(m, d)
    row = lambda a: a.reshape(1, -1)
    for l in range(depth):
        w_main, w_vt, w_idx = _pack_w_in(w_in[l])
        g = row(attn_norm[l])
        p_main = norm_matmul(xc, g, w_main, tm=1024, tn=1024)
        p_idx = norm_matmul(xc, g, w_idx, tm=1024, tn=IDX_COLS)
        vt = norm_matmul_t(xc, g, w_vt, seq, tm=512)
        qn, kn, qip, kil, wis = dsa_prep(p_main, p_idx, rope, row(q_norm[l]), row(k_norm[l]), seq, ts=512)
        att = dsa_attention(qn, qip, wis, kil, kn, vt, _logits_bounded(q_norm[l], k_norm[l]),
                            batch, seq, topk)
        mk, mv = mem_kv(mem2d, row(mem_norm[l]), w_mem_kv[l].astype(BF16), row(mk_norm[l]), batch, n_mem)
        memo = mem_attention(p_main, mk, mv, row(mq_norm[l]), seq, n_mem, tm=512)
        xc = merge(xc, p_main, att, memo, conv_w[l], w_br_conv[l].astype(BF16), w_br_attn[l].astype(BF16),
                   w_br_mem[l].astype(BF16), w_out[l].astype(BF16), seq, tm=256)
        j = l // 2
        if l % 2 == 0:
            xc = ffn_dense(xc, row(ffn_norm[l]), ff_gate[j].astype(BF16), ff_up[j].astype(BF16),
                           ff_down[j].astype(BF16), tm=512, tf=512)
        else:
            xc = moe_layer(xc, row(ffn_norm[l]), router[j], ex_gate[j].astype(BF16), ex_up[j].astype(BF16),
                           ex_down[j].astype(BF16))
    return xc.reshape(batch, seq, d)
```

```python
import functools
import math

import numpy as np
import jax
import jax.numpy as jnp
from jax import lax
from jax.experimental import pallas as pl
from jax.experimental.pallas import tpu as pltpu

F32 = jnp.float32
BF16 = jnp.bfloat16
I32 = jnp.int32

N_HEADS = 8
HEAD_DIM = 128
N_IDX_HEADS = 8
IDX_DIM = 64
MAX_TOPK = 256
N_MEM_HEADS = 4
MEM_HEAD_DIM = 256
N_EXPERTS = 8
TOP_K = 2
CONV_WIDTH = 3
ROPE_THETA = 10000.0
EPS = 1e-6

LANES = 128
SUBLANES = 8
INT_MIN = -(2 ** 31)
INT_MAX = 2 ** 31 - 1
LOG2E = math.log2(math.e)
VMEM_LIMIT = 56 * 1024 * 1024

COL_CB, COL_CC, COL_CU, COL_Q, COL_K, COL_QM, COL_GC, COL_GA, COL_GM = range(9)
IDX_COLS = 640


def _cparams(*sem):
    return pltpu.CompilerParams(dimension_semantics=sem, vmem_limit_bytes=VMEM_LIMIT)


def _rms(x, g):
    return x * lax.rsqrt(jnp.mean(x * x, axis=-1, keepdims=True) + EPS) * g


def _norm_matmul_kernel(x_ref, g_ref, w_ref, o_ref, h_ref):
    @pl.when(pl.program_id(1) == 0)
    def _():
        h_ref[...] = _rms(x_ref[...], g_ref[...]).astype(BF16)

    o_ref[...] = jnp.dot(h_ref[...], w_ref[...], preferred_element_type=F32)


def norm_matmul(x, g, w, tm, tn):
    m, k = x.shape
    n = w.shape[1]
    return pl.pallas_call(
        _norm_matmul_kernel,
        grid=(m // tm, n // tn),
        in_specs=[pl.BlockSpec((tm, k), lambda i, j: (i, 0)),
                  pl.BlockSpec((1, k), lambda i, j: (0, 0)),
                  pl.BlockSpec((k, tn), lambda i, j: (0, j))],
        out_specs=pl.BlockSpec((tm, tn), lambda i, j: (i, j)),
        out_shape=jax.ShapeDtypeStruct((m, n), F32),
        scratch_shapes=[pltpu.VMEM((tm, k), BF16)],
        compiler_params=_cparams("parallel", "arbitrary"),
        name="norm_matmul",
    )(x, g, w)


def _norm_matmul_t_kernel(x_ref, g_ref, wt_ref, o_ref):
    h = _rms(x_ref[...], g_ref[...]).astype(BF16)
    o_ref[...] = lax.dot_general(wt_ref[...], h, (((1,), (1,)), ((), ())),
                                 preferred_element_type=F32).astype(o_ref.dtype)


def norm_matmul_t(x, g, wt, seq, tm):
    m, k = x.shape
    n = wt.shape[0]
    tiles_per_seq = seq // tm
    return pl.pallas_call(
        _norm_matmul_t_kernel,
        grid=(m // tm,),
        in_specs=[pl.BlockSpec((tm, k), lambda i: (i, 0)),
                  pl.BlockSpec((1, k), lambda i: (0, 0)),
                  pl.BlockSpec((n, k), lambda i: (0, 0))],
        out_specs=pl.BlockSpec((n, tm), lambda i: (i // tiles_per_seq, i % tiles_per_seq)),
        out_shape=jax.ShapeDtypeStruct((m // seq * n, seq), BF16),
        compiler_params=_cparams("parallel"),
        name="norm_matmul_t",
    )(x, g, wt)


def _prep_kernel(q_ref, k_ref, idx_ref, ca_ref, sa_ref, ci_ref, si_ref, qg_ref, kg_ref,
                 qn_ref, kn_ref, qip_ref, kil_ref, wis_ref, *, q_scale, w_scale):
    ca = ca_ref[...]
    sa = sa_ref[...]
    for h in range(N_HEADS):
        sl = slice(h * HEAD_DIM, (h + 1) * HEAD_DIM)
        for src, g_ref, dst, scale in ((q_ref, qg_ref, qn_ref, q_scale), (k_ref, kg_ref, kn_ref, 1.0)):
            n = _rms(src[:, sl], g_ref[...])
            o = n * ca + pltpu.roll(n, HEAD_DIM // 2, 1) * sa
            dst[:, sl] = (o * scale).astype(BF16)

    ci = ci_ref[...]
    si = si_ref[...]
    lane = lax.broadcasted_iota(I32, ci.shape, 1)
    first_half = (lane & (IDX_DIM // 2)) == 0
    low_head = lane < IDX_DIM

    def idx_rope(x):
        rot = jnp.where(first_half, pltpu.roll(x, LANES - IDX_DIM // 2, 1), pltpu.roll(x, IDX_DIM // 2, 1))
        return x * ci + rot * si

    for c in range(N_IDX_HEADS // 2):
        o = idx_rope(idx_ref[:, c * LANES:(c + 1) * LANES])
        qip_ref[:, (2 * c) * LANES:(2 * c + 1) * LANES] = jnp.where(low_head, o, 0.0).astype(BF16)
        qip_ref[:, (2 * c + 1) * LANES:(2 * c + 2) * LANES] = jnp.where(
            low_head, pltpu.roll(o, IDX_DIM, 1), 0.0).astype(BF16)
    tail = idx_ref[:, N_IDX_HEADS * IDX_DIM:N_IDX_HEADS * IDX_DIM + LANES]
    kil_ref[...] = jnp.where(low_head, idx_rope(tail), 0.0).astype(BF16)
    wis_ref[...] = tail * w_scale


def dsa_prep(p_main, p_idx, rope, q_norm, k_norm, seq, ts):
    m = p_main.shape[0]
    d = N_HEADS * HEAD_DIM
    n_seq_tiles = seq // ts
    ca, sa, ci, si = rope
    row = lambda c: pl.BlockSpec((ts, d), lambda i, c=c: (i, c))
    tab = pl.BlockSpec((ts, LANES), lambda i: (i % n_seq_tiles, 0))
    gain = pl.BlockSpec((1, HEAD_DIM), lambda i: (0, 0))
    out_d = pl.BlockSpec((ts, d), lambda i: (i, 0))
    out_l = pl.BlockSpec((ts, LANES), lambda i: (i, 0))
    q_scale = HEAD_DIM ** -0.5 * LOG2E
    w_scale = (N_IDX_HEADS * IDX_DIM) ** -0.5
    return pl.pallas_call(
        functools.partial(_prep_kernel, q_scale=q_scale, w_scale=w_scale),
        grid=(m // ts,),
        in_specs=[row(COL_Q), row(COL_K),
                  pl.BlockSpec((ts, IDX_COLS), lambda i: (i, 0)),
                  tab, tab, tab, tab, gain, gain],
        out_specs=[out_d, out_d, out_d, out_l, out_l],
        out_shape=[jax.ShapeDtypeStruct((m, d), BF16)] * 3
        + [jax.ShapeDtypeStruct((m, LANES), BF16), jax.ShapeDtypeStruct((m, LANES), F32)],
        compiler_params=_cparams("parallel"),
        name="dsa_prep",
    )(p_main, p_main, p_idx, ca, sa, ci, si, q_norm, k_norm)


SCORE_ROWS = 512
SLAB = 32
NT_DIMS = (((1,), (1,)), ((), ()))
ZERO_SCORE_BASE = 2 ** 23 - 1


def _dsa_kernel(qi_of, kb_of, last_of, nsc_of, nblk_of, nfull_of, bounded,
                qn_ref, qip_ref, wis_ref, kil_ref, kn_ref, vt_ref,
                o_ref,
                keys_ref, thr_ref, w_ref, m_ref, l_ref, acc_ref, s_ref, gmax_ref,
                *, tq, tk, topk):
    p = pl.program_id(1)
    qi = qi_of[p]
    kb = kb_of[p]

    @pl.when(kb == 0)
    def _():
        w_ref[...] = wis_ref[...].T[IDX_DIM:IDX_DIM + N_IDX_HEADS, :]
        q_id = qi * tq + lax.broadcasted_iota(I32, (SCORE_ROWS, tq), 1)
        k_iota = lax.broadcasted_iota(I32, (SCORE_ROWS, tq), 0)
        n_sc = nsc_of[p]

        def score_chunk(c, on_diagonal):
            start = pl.multiple_of(c * SCORE_ROWS, SCORE_ROWS)
            kblk = kil_ref[pl.ds(start, SCORE_ROWS), :]
            sc = jnp.zeros((SCORE_ROWS, tq), F32)
            for h in range(N_IDX_HEADS):
                z = lax.dot_general(kblk, qip_ref[:, h * LANES:(h + 1) * LANES], NT_DIMS,
                                    preferred_element_type=F32)
                sc = sc + jnp.maximum(z, 0.0) * w_ref[h:h + 1, :]
            bits = pltpu.bitcast(sc, I32)
            key = jnp.where(bits < 0, bits ^ 0x7FFFFFFF, bits)
            k_id = k_iota + start
            key = jnp.where(key == 0, ZERO_SCORE_BASE - k_id, key)
            if on_diagonal:
                key = jnp.where(k_id <= q_id, key, INT_MIN)
            keys_ref[pl.ds(start, SCORE_ROWS), :] = key
            return key

        def score_block(t, carry, on_diagonal):
            gmax = gmax_ref[...]
            for u in range(tk // SCORE_ROWS):
                key = score_chunk(t * (tk // SCORE_ROWS) + u, on_diagonal)
                for j in range(SCORE_ROWS // MAX_TOPK):
                    gmax = jnp.maximum(gmax, key[j * MAX_TOPK:(j + 1) * MAX_TOPK, :])
            gmax_ref[...] = gmax
            return carry

        gmax_ref[...] = jnp.full(gmax_ref.shape, INT_MIN, I32)
        lax.fori_loop(0, nfull_of[p], functools.partial(score_block, on_diagonal=False), 0)
        lax.fori_loop(nfull_of[p], nblk_of[p], functools.partial(score_block, on_diagonal=True), 0)

        def count_ge(cand):
            def count_step(c, acc):
                start = pl.multiple_of(c * SCORE_ROWS, SCORE_ROWS)
                for j in range(SCORE_ROWS // SLAB):
                    acc = acc + (keys_ref[pl.ds(start + j * SLAB, SLAB), :] >= cand).astype(I32)
                return acc

            acc = lax.fori_loop(0, n_sc, count_step, jnp.zeros((SLAB, tq), I32))
            return jnp.sum(acc.astype(F32), axis=0, keepdims=True)

        gmax = gmax_ref[...]
        lo0 = jnp.maximum(jnp.min(gmax, axis=0, keepdims=True), INT_MIN + 1)
        hi0 = jnp.minimum(jnp.max(gmax, axis=0, keepdims=True), INT_MAX - 1) + 1

        def open_interval(lo, hi):
            return jnp.max(((lo + 1) < hi).astype(I32)) > 0

        def halve(lo, hi):
            mid = (lo & hi) + ((lo ^ hi) >> 1)
            cnt = count_ge(mid)
            keep = cnt >= topk
            lo = jnp.where(keep, mid, lo)
            hi = jnp.where(cnt == topk, mid + 1, jnp.where(keep, hi, mid))
            return lo, hi

        def bisect(state):
            lo, hi = halve(*halve(state[0], state[1]))
            return lo, hi, open_interval(lo, hi)

        lo, _, _ = lax.while_loop(lambda st: st[2], bisect, (lo0, hi0, open_interval(lo0, hi0)))
        thr_ref[...] = lo

        m_ref[...] = jnp.full(m_ref.shape, -jnp.inf, F32)
        l_ref[...] = jnp.zeros(l_ref.shape, F32)
        acc_ref[...] = jnp.zeros(acc_ref.shape, F32)

    def mask_bias():
        keys = keys_ref[pl.ds(pl.multiple_of(kb * tk, tk), tk), :]
        return jnp.where(keys >= thr_ref[...], 0.0, -jnp.inf).astype(F32)

    def masked_logits(h, bias):
        sl = slice(h * HEAD_DIM, (h + 1) * HEAD_DIM)
        return lax.dot_general(kn_ref[:, sl], qn_ref[:, sl], NT_DIMS, preferred_element_type=F32) + bias


    @pl.when(bounded[0] == 1)
    def _():
        bias = mask_bias()
        s_ref[0] = masked_logits(0, bias)
        for h in range(N_HEADS):
            sl = slice(h * HEAD_DIM, (h + 1) * HEAD_DIM)
            if h + 1 < N_HEADS:
                s_ref[(h + 1) % 2] = masked_logits(h + 1, bias)
            pr = jnp.exp2(s_ref[h % 2])
            l_ref[h:h + 1, :] = l_ref[h:h + 1, :] + jnp.sum(pr, axis=0, keepdims=True)
            acc_ref[sl, :] = acc_ref[sl, :] + jnp.dot(
                vt_ref[sl, :], pr.astype(BF16), preferred_element_type=F32)

    @pl.when(bounded[0] == 0)
    def _():
        bias = mask_bias()

        def logits(h):
            s = masked_logits(h, bias)
            s_ref[h % 2] = s
            return jnp.max(s, axis=0, keepdims=True)

        def accumulate(h, m_cur):
            sl = slice(h * HEAD_DIM, (h + 1) * HEAD_DIM)
            m_prev = m_ref[h:h + 1, :]
            m_new = jnp.maximum(m_prev, m_cur)
            m_safe = jnp.where(m_new == -jnp.inf, 0.0, m_new)
            alpha = jnp.exp2(m_prev - m_safe)
            pr = jnp.exp2(s_ref[h % 2] - m_safe)
            l_ref[h:h + 1, :] = alpha * l_ref[h:h + 1, :] + jnp.sum(pr, axis=0, keepdims=True)
            m_ref[h:h + 1, :] = m_new
            acc_ref[sl, :] = acc_ref[sl, :] * alpha + jnp.dot(
                vt_ref[sl, :], pr.astype(BF16), preferred_element_type=F32)

        m_cur = logits(0)
        for h in range(N_HEADS):
            m_next = logits(h + 1) if h + 1 < N_HEADS else None
            accumulate(h, m_cur)
            m_cur = m_next

    @pl.when(last_of[p] == 1)
    def _():
        for h in range(N_HEADS):
            sl = slice(h * HEAD_DIM, (h + 1) * HEAD_DIM)
            o_ref[:, sl] = (acc_ref[sl, :] / l_ref[h:h + 1, :]).T.astype(o_ref.dtype)


def _dsa_schedule(seq, tq, tk):
    assert tk % SCORE_ROWS == 0
    qi_of, kb_of, last_of, nsc_of, nblk_of, nfull_of = [], [], [], [], [], []
    for qi in range(seq // tq):
        last_kb = ((qi + 1) * tq - 1) // tk
        for kb in range(last_kb + 1):
            qi_of.append(qi)
            kb_of.append(kb)
            last_of.append(int(kb == last_kb))
            nsc_of.append(-(-((qi + 1) * tq) // SCORE_ROWS))
            nblk_of.append(last_kb + 1)
            nfull_of.append((qi * tq + 1) // tk)
    return tuple(np.asarray(a, np.int32) for a in (qi_of, kb_of, last_of, nsc_of, nblk_of, nfull_of))


def dsa_attention(qn, qip, wis, kil, kn, vt, bounded, batch, seq, topk, tq=256, tk=1024):
    d = N_HEADS * HEAD_DIM
    nq = seq // tq
    nk = seq // tk
    assert seq <= ZERO_SCORE_BASE and topk <= MAX_TOPK
    sched = _dsa_schedule(seq, tq, tk)
    n_steps = len(sched[0])
    qrow = lambda w: pl.BlockSpec((tq, w), lambda b, p, qi_of, kb_of, *_: (b * nq + qi_of[p], 0))
    grid_spec = pltpu.PrefetchScalarGridSpec(
        num_scalar_prefetch=7,
        grid=(batch, n_steps),
        in_specs=[qrow(d), qrow(d), qrow(LANES),
                  pl.BlockSpec((seq, LANES), lambda b, p, *_: (b, 0)),
                  pl.BlockSpec((tk, d), lambda b, p, qi_of, kb_of, *_: (b * nk + kb_of[p], 0)),
                  pl.BlockSpec((d, tk), lambda b, p, qi_of, kb_of, *_: (b, kb_of[p]))],
        out_specs=qrow(d),
        scratch_shapes=[pltpu.VMEM((seq, tq), I32),
                        pltpu.VMEM((1, tq), I32),
                        pltpu.VMEM((N_IDX_HEADS, tq), F32),
                        pltpu.VMEM((N_HEADS, tq), F32),
                        pltpu.VMEM((N_HEADS, tq), F32),
                        pltpu.VMEM((d, tq), F32),
                        pltpu.VMEM((2, tk, tq), F32),
                        pltpu.VMEM((MAX_TOPK, tq), I32)],
    )
    return pl.pallas_call(
        functools.partial(_dsa_kernel, tq=tq, tk=tk, topk=topk),
        grid_spec=grid_spec,
        out_shape=jax.ShapeDtypeStruct((batch * seq, d), BF16),
        compiler_params=_cparams("parallel", "arbitrary"),
        name="dsa_attention",
    )(*sched, bounded, qn, qip, wis, kil, kn, vt)


LOGIT_BOUND = 100.0


def _logits_bounded(q_gain, k_gain):
    bound = HEAD_DIM * jnp.max(jnp.abs(q_gain)) * jnp.max(jnp.abs(k_gain)) * (HEAD_DIM ** -0.5 * LOG2E)
    return (bound <= LOGIT_BOUND).astype(I32).reshape(1)


def _mem_kv_kernel(mem_ref, g_ref, w_ref, kg_ref, mk_ref, mv_ref):
    d = N_MEM_HEADS * MEM_HEAD_DIM
    h = _rms(mem_ref[...], g_ref[...]).astype(BF16)
    kv = jnp.dot(h, w_ref[...], preferred_element_type=F32)
    for hd in range(N_MEM_HEADS):
        sl = slice(hd * MEM_HEAD_DIM, (hd + 1) * MEM_HEAD_DIM)
        mk_ref[:, sl] = _rms(kv[:, sl], kg_ref[...]).astype(BF16)
    mv_ref[...] = kv[:, d:].astype(BF16)


def mem_kv(mem2d, mem_norm, w_mem_kv, mk_norm, batch, n_mem):
    dm = mem2d.shape[1]
    d = N_MEM_HEADS * MEM_HEAD_DIM
    blk = pl.BlockSpec((n_mem, d), lambda b: (b, 0))
    return pl.pallas_call(
        _mem_kv_kernel,
        grid=(batch,),
        in_specs=[pl.BlockSpec((n_mem, dm), lambda b: (b, 0)),
                  pl.BlockSpec((1, dm), lambda b: (0, 0)),
                  pl.BlockSpec((dm, 2 * d), lambda b: (0, 0)),
                  pl.BlockSpec((1, MEM_HEAD_DIM), lambda b: (0, 0))],
        out_specs=[blk, blk],
        out_shape=[jax.ShapeDtypeStruct((batch * n_mem, d), BF16)] * 2,
        compiler_params=_cparams("arbitrary"),
        name="mem_kv",
    )(mem2d, mem_norm, w_mem_kv, mk_norm)


def _mem_attn_kernel(q_ref, mk_ref, mv_ref, g_ref, o_ref, *, scale):
    for hd in range(N_MEM_HEADS):
        sl = slice(hd * MEM_HEAD_DIM, (hd + 1) * MEM_HEAD_DIM)
        qh = (_rms(q_ref[:, sl], g_ref[...]) * scale).astype(BF16)
        s = lax.dot_general(qh, mk_ref[:, sl], (((1,), (1,)), ((), ())), preferred_element_type=F32)
        pr = jnp.exp2(s - jnp.max(s, axis=1, keepdims=True))
        o = jnp.dot(pr.astype(BF16), mv_ref[:, sl], preferred_element_type=F32)
        o_ref[:, sl] = (o / jnp.sum(pr, axis=1, keepdims=True)).astype(o_ref.dtype)


def mem_attention(p_main, mk, mv, mq_norm, seq, n_mem, tm):
    m = p_main.shape[0]
    d = N_MEM_HEADS * MEM_HEAD_DIM
    tiles_per_seq = seq // tm
    kv = pl.BlockSpec((n_mem, d), lambda i: (i // tiles_per_seq, 0))
    return pl.pallas_call(
        functools.partial(_mem_attn_kernel, scale=MEM_HEAD_DIM ** -0.5 * LOG2E),
        grid=(m // tm,),
        in_specs=[pl.BlockSpec((tm, d), lambda i: (i, COL_QM)), kv, kv,
                  pl.BlockSpec((1, MEM_HEAD_DIM), lambda i: (0, 0))],
        out_specs=pl.BlockSpec((tm, d), lambda i: (i, 0)),
        out_shape=jax.ShapeDtypeStruct((m, d), BF16),
        compiler_params=_cparams("parallel"),
        name="mem_attention",
    )(p_main, mk, mv, mq_norm)


HALO = 8


def _merge_kernel(x_ref, cb_ref, cc_ref, cu_ref, cch_ref, cuh_ref, gc_ref, ga_ref, gm_ref,
                  att_ref, mem_ref, cw_ref, wc_ref, wa_ref, wm_ref, wo_ref, o_ref, *, tiles_per_seq):
    i = pl.program_id(0)
    pcur = cc_ref[...] * cu_ref[...]
    halo = jnp.where(i % tiles_per_seq == 0, 0.0, cch_ref[...] * cuh_ref[...])
    row = lax.broadcasted_iota(I32, pcur.shape, 0)
    h1 = halo[HALO - 1:HALO, :]
    h2 = halo[HALO - 2:HALO - 1, :]
    p1 = jnp.where(row == 0, h1, pltpu.roll(pcur, 1, 0))
    p2 = jnp.where(row == 0, h2, jnp.where(row == 1, h1, pltpu.roll(pcur, 2, 0)))
    cw = cw_ref[...]
    conv = cb_ref[...] * (cw[0:1, :] * p2 + cw[1:2, :] * p1 + cw[2:3, :] * pcur)
    merged = jax.nn.sigmoid(gc_ref[...]) * jnp.dot(conv.astype(BF16), wc_ref[...], preferred_element_type=F32)
    merged += jax.nn.sigmoid(ga_ref[...]) * jnp.dot(att_ref[...], wa_ref[...], preferred_element_type=F32)
    merged += jax.nn.sigmoid(gm_ref[...]) * jnp.dot(mem_ref[...], wm_ref[...], preferred_element_type=F32)
    o_ref[...] = x_ref[...] + jnp.dot(merged.astype(BF16), wo_ref[...], preferred_element_type=F32)


def merge(x2d, p_main, att, memo, conv_w, w_c, w_a, w_m, w_o, seq, tm):
    m, d = x2d.shape
    tiles_per_seq = seq // tm
    col = lambda c: pl.BlockSpec((tm, d), lambda i, c=c: (i, c))
    halo = lambda c: pl.BlockSpec((HALO, d), lambda i, c=c: (jnp.maximum(i * (tm // HALO) - 1, 0), c))
    rowb = pl.BlockSpec((tm, d), lambda i: (i, 0))
    wspec = pl.BlockSpec((d, d), lambda i: (0, 0))
    return pl.pallas_call(
        functools.partial(_merge_kernel, tiles_per_seq=tiles_per_seq),
        grid=(m // tm,),
        in_specs=[rowb, col(COL_CB), col(COL_CC), col(COL_CU), halo(COL_CC), halo(COL_CU),
                  col(COL_GC), col(COL_GA), col(COL_GM), rowb, rowb,
                  pl.BlockSpec((CONV_WIDTH, d), lambda i: (0, 0)),
                  wspec, wspec, wspec, wspec],
        out_specs=rowb,
        out_shape=jax.ShapeDtypeStruct((m, d), F32),
        compiler_params=_cparams("parallel"),
        name="merge",
    )(x2d, p_main, p_main, p_main, p_main, p_main, p_main, p_main, p_main, att, memo,
      conv_w, w_c, w_a, w_m, w_o)


def _ffn_kernel(x_ref, g_ref, wg_ref, wu_ref, wd_ref, o_ref, h_ref, acc_ref):
    j = pl.program_id(1)

    @pl.when(j == 0)
    def _():
        h_ref[...] = _rms(x_ref[...], g_ref[...]).astype(BF16)
        acc_ref[...] = jnp.zeros(acc_ref.shape, F32)

    h = h_ref[...]
    a = jax.nn.silu(jnp.dot(h, wg_ref[...], preferred_element_type=F32)) * jnp.dot(
        h, wu_ref[...], preferred_element_type=F32)
    acc_ref[...] += jnp.dot(a.astype(BF16), wd_ref[...], preferred_element_type=F32)

    @pl.when(j == pl.num_programs(1) - 1)
    def _():
        o_ref[...] = x_ref[...] + acc_ref[...]


def ffn_dense(x2d, g, w_gate, w_up, w_down, tm, tf):
    m, d = x2d.shape
    ff = w_gate.shape[1]
    return pl.pallas_call(
        _ffn_kernel,
        grid=(m // tm, ff // tf),
        in_specs=[pl.BlockSpec((tm, d), lambda i, j: (i, 0)),
                  pl.BlockSpec((1, d), lambda i, j: (0, 0)),
                  pl.BlockSpec((d, tf), lambda i, j: (0, j)),
                  pl.BlockSpec((d, tf), lambda i, j: (0, j)),
                  pl.BlockSpec((tf, d), lambda i, j: (j, 0))],
        out_specs=pl.BlockSpec((tm, d), lambda i, j: (i, 0)),
        out_shape=jax.ShapeDtypeStruct((m, d), F32),
        scratch_shapes=[pltpu.VMEM((tm, d), BF16), pltpu.VMEM((tm, d), F32)],
        compiler_params=_cparams("parallel", "arbitrary"),
        name="ffn_dense",
    )(x2d, g, w_gate, w_up, w_down)


def _router_kernel(x_ref, g_ref, r_ref, h_ref, eid_ref, gw_ref):
    h = _rms(x_ref[...], g_ref[...])
    _rows_to_tiles(h_ref, h)
    logits = jnp.dot(h, r_ref[...], preferred_element_type=F32, precision=lax.Precision.HIGHEST)
    lane = lax.broadcasted_iota(I32, logits.shape, 1)
    lanef = lane.astype(F32)
    logits = jnp.where(lane < N_EXPERTS, logits, -jnp.inf)
    m1 = jnp.max(logits, axis=1, keepdims=True)
    i1 = jnp.min(jnp.where(logits == m1, lanef, float(LANES)), axis=1, keepdims=True)
    rest = jnp.where(lanef == i1, -jnp.inf, logits)
    m2 = jnp.max(rest, axis=1, keepdims=True)
    i2 = jnp.min(jnp.where(rest == m2, lanef, float(LANES)), axis=1, keepdims=True)
    e = jnp.exp(m2 - m1)
    g1 = 1.0 / (1.0 + e)
    eid_ref[...] = jnp.where(lane == 0, i1, jnp.where(lane == 1, i2, 0.0)).astype(I32)
    gw_ref[...] = jnp.where(lane == 0, g1, jnp.where(lane == 1, e * g1, 0.0))


def router(x2d, g, router_pad, tm):
    m, d = x2d.shape
    assert d == SUBLANES * LANES
    lane_out = pl.BlockSpec((tm, LANES), lambda i: (i, 0))
    return pl.pallas_call(
        _router_kernel,
        grid=(m // tm,),
        in_specs=[pl.BlockSpec((tm, d), lambda i: (i, 0)),
                  pl.BlockSpec((1, d), lambda i: (0, 0)),
                  pl.BlockSpec((d, LANES), lambda i: (0, 0))],
        out_specs=[pl.BlockSpec((tm * SUBLANES, LANES), lambda i: (i, 0)), lane_out, lane_out],
        out_shape=[jax.ShapeDtypeStruct((m * SUBLANES, LANES), F32),
                   jax.ShapeDtypeStruct((m, LANES), I32),
                   jax.ShapeDtypeStruct((m, LANES), F32)],
        compiler_params=_cparams("parallel"),
        name="router",
    )(x2d, g, router_pad)


GATHER_UNROLL = 8


def _rows_to_tiles(ref, val):
    n = val.shape[0]
    for c in range(SUBLANES):
        ref[pl.ds(c, n, stride=SUBLANES), :] = val[:, c * LANES:(c + 1) * LANES]


def _tiles_to_rows(ref, first, n):
    return jnp.concatenate(
        [ref[pl.ds(first * SUBLANES + c, n, stride=SUBLANES), :] for c in range(SUBLANES)], axis=1)


def _tile_copy(src_ref, src_tile, dst_ref, dst_tile, sem):
    return pltpu.make_async_copy(src_ref.at[pl.ds(pl.multiple_of(src_tile * SUBLANES, SUBLANES), SUBLANES)],
                                 dst_ref.at[pl.ds(pl.multiple_of(dst_tile * SUBLANES, SUBLANES), SUBLANES)], sem)


def _start_tile_gather(src_ref, idx_ref, dst_ref, sem, n_tiles):
    def trip(g, c):
        for u in range(GATHER_UNROLL):
            r = g * GATHER_UNROLL + u
            _tile_copy(src_ref, idx_ref[0, 0, r], dst_ref, r, sem).start()
        return c

    lax.fori_loop(0, n_tiles // GATHER_UNROLL, trip, 0)


def _wait_tile_gather(src_ref, dst_ref, sem, n_tiles):
    def trip(g, c):
        for u in range(GATHER_UNROLL):
            _tile_copy(src_ref, 0, dst_ref, g * GATHER_UNROLL + u, sem).wait()
        return c

    lax.fori_loop(0, n_tiles // GATHER_UNROLL, trip, 0)


def _expert_kernel(blk_exp, n_used, tok_ref, tok_next_ref, h_ref, wg_ref, wu_ref, wd_ref, y_ref,
                   xbuf, xb_ref, acc_ref, sem, *, mb):
    b = pl.program_id(0)
    j = pl.program_id(1)
    active = b < n_used[0]
    last = j == pl.num_programs(1) - 1
    slot = b % 2

    def gather(idx_ref, s):
        _start_tile_gather(h_ref, idx_ref, xbuf.at[s], sem.at[s], mb)

    @pl.when(jnp.logical_and(b == 0, j == 0))
    def _():
        gather(tok_ref, 0)

    @pl.when(jnp.logical_and(active, j == 0))
    def _():
        _wait_tile_gather(h_ref, xbuf.at[slot], sem.at[slot], mb)
        xb_ref[...] = _tiles_to_rows(xbuf.at[slot], 0, mb).astype(BF16)
        acc_ref[...] = jnp.zeros(acc_ref.shape, F32)

    @pl.when(jnp.logical_and(b + 1 < n_used[0], j == 1))
    def _():
        gather(tok_next_ref, 1 - slot)

    @pl.when(active)
    def _():
        x = xb_ref[...]
        a = jax.nn.silu(jnp.dot(x, wg_ref[0], preferred_element_type=F32)) * jnp.dot(
            x, wu_ref[0], preferred_element_type=F32)
        acc_ref[...] += jnp.dot(a.astype(BF16), wd_ref[0], preferred_element_type=F32)

    @pl.when(jnp.logical_and(active, last))
    def _():
        _rows_to_tiles(y_ref, acc_ref[...])

    @pl.when(jnp.logical_and(jnp.logical_not(active), last))
    def _():
        y_ref[...] = jnp.zeros(y_ref.shape, F32)


def expert_ffn(hf, slot_tok, blk_exp, n_used, ex_gate, ex_up, ex_down, mb, tf):
    d = hf.shape[1] * SUBLANES
    n_blk = blk_exp.shape[0]
    ff = ex_gate.shape[2]
    assert ff // tf >= 2
    tok3 = slot_tok.reshape(n_blk, 1, mb)
    grid_spec = pltpu.PrefetchScalarGridSpec(
        num_scalar_prefetch=2,
        grid=(n_blk, ff // tf),
        in_specs=[pl.BlockSpec((1, 1, mb), lambda b, j, be, nu: (b, 0, 0), memory_space=pltpu.SMEM),
                  pl.BlockSpec((1, 1, mb), lambda b, j, be, nu: (jnp.minimum(b + 1, n_blk - 1), 0, 0),
                               memory_space=pltpu.SMEM),
                  pl.BlockSpec(memory_space=pl.ANY),
                  pl.BlockSpec((1, d, tf), lambda b, j, be, nu: (be[b], 0, j)),
                  pl.BlockSpec((1, d, tf), lambda b, j, be, nu: (be[b], 0, j)),
                  pl.BlockSpec((1, tf, d), lambda b, j, be, nu: (be[b], j, 0))],
        out_specs=pl.BlockSpec((mb * SUBLANES, LANES), lambda b, j, be, nu: (b, 0)),
        scratch_shapes=[pltpu.VMEM((2, mb * SUBLANES, LANES), F32), pltpu.VMEM((mb, d), BF16),
                        pltpu.VMEM((mb, d), F32), pltpu.SemaphoreType.DMA((2,))],
    )
    return pl.pallas_call(
        functools.partial(_expert_kernel, mb=mb),
        grid_spec=grid_spec,
        out_shape=jax.ShapeDtypeStruct((n_blk * mb * SUBLANES, LANES), F32),
        compiler_params=_cparams("arbitrary", "arbitrary"),
        name="expert_ffn",
    )(blk_exp, n_used, tok3, tok3, hf, ex_gate, ex_up, ex_down)


def _combine_kernel(pos_ref, pos_next_ref, x_ref, gw_ref, ys_ref, o_ref, buf, sem, *, tm):
    i = pl.program_id(0)
    slot = i % 2
    n_rows = TOP_K * tm

    def gather(idx_ref, s):
        _start_tile_gather(ys_ref, idx_ref, buf.at[s], sem.at[s], n_rows)

    @pl.when(i == 0)
    def _():
        gather(pos_ref, 0)

    @pl.when(i + 1 < pl.num_programs(0))
    def _():
        gather(pos_next_ref, 1 - slot)

    _wait_tile_gather(ys_ref, buf.at[slot], sem.at[slot], n_rows)
    out = x_ref[...]
    gw = gw_ref[...]
    for kk in range(TOP_K):
        out = out + _tiles_to_rows(buf.at[slot], kk * tm, tm) * gw[:, kk:kk + 1]
    o_ref[...] = out


def moe_combine(x2d, gw, ys, pos, tm):
    m, d = x2d.shape
    n_tiles = m // tm
    return pl.pallas_call(
        functools.partial(_combine_kernel, tm=tm),
        grid=(n_tiles,),
        in_specs=[pl.BlockSpec((1, 1, TOP_K * tm), lambda i: (i, 0, 0), memory_space=pltpu.SMEM),
                  pl.BlockSpec((1, 1, TOP_K * tm), lambda i: (jnp.minimum(i + 1, n_tiles - 1), 0, 0),
                               memory_space=pltpu.SMEM),
                  pl.BlockSpec((tm, d), lambda i: (i, 0)),
                  pl.BlockSpec((tm, LANES), lambda i: (i, 0)),
                  pl.BlockSpec(memory_space=pl.ANY)],
        out_specs=pl.BlockSpec((tm, d), lambda i: (i, 0)),
        out_shape=jax.ShapeDtypeStruct((m, d), F32),
        scratch_shapes=[pltpu.VMEM((2, TOP_K * tm * SUBLANES, LANES), F32), pltpu.SemaphoreType.DMA((2,))],
        compiler_params=_cparams("arbitrary"),
        name="moe_combine",
    )(pos, pos, x2d, gw, ys)


def _moe_plan(eid, n_tok, mb):
    n_asg = n_tok * TOP_K
    n_blk = n_asg // mb + N_EXPERTS
    cap = n_blk * mb
    e = eid.reshape(-1)
    tok = jnp.repeat(jnp.arange(n_tok, dtype=I32), TOP_K)
    onehot = (e[:, None] == jnp.arange(N_EXPERTS, dtype=I32)[None, :]).astype(I32)
    rank = jnp.cumsum(onehot, axis=0) - onehot
    rank = jnp.sum(rank * onehot, axis=1)
    counts = jnp.sum(onehot, axis=0)
    padded = (counts + mb - 1) // mb * mb
    pad_end = jnp.cumsum(padded)
    pad_start = pad_end - padded
    dest = (pad_start[e] + rank).astype(I32)
    slot_tok = jnp.zeros((cap,), I32).at[dest].set(tok)
    blk_start = jnp.arange(n_blk, dtype=I32) * mb
    blk_exp = jnp.minimum(jnp.searchsorted(pad_end, blk_start, side='right'), N_EXPERTS - 1).astype(I32)
    n_used = (pad_end[-1] // mb).astype(I32).reshape(1)
    return slot_tok, blk_exp, n_used, dest.reshape(n_tok, TOP_K)


def moe_layer(x2d, g, router_w, ex_gate, ex_up, ex_down, mb=512, tf=1792, tm_route=512, tm_comb=256):
    n_tok, d = x2d.shape
    router_pad = jnp.zeros((d, LANES), F32).at[:, :N_EXPERTS].set(router_w)
    hf, eid, gw = router(x2d, g, router_pad, tm_route)
    slot_tok, blk_exp, n_used, dest = _moe_plan(eid[:, :TOP_K], n_tok, mb)
    ys = expert_ffn(hf, slot_tok, blk_exp, n_used, ex_gate, ex_up, ex_down, mb, tf)
    pos = dest.reshape(n_tok // tm_comb, tm_comb, TOP_K).transpose(0, 2, 1).reshape(
        n_tok // tm_comb, 1, TOP_K * tm_comb)
    return moe_combine(x2d, gw, ys, pos, tm_comb)


def _rope_tables(seq):
    def tab(dim):
        inv = 1.0 / (ROPE_THETA ** (jnp.arange(0, dim, 2, dtype=F32) / dim))
        ang = jnp.arange(seq, dtype=F32)[:, None] * inv[None, :]
        cos, sin = jnp.cos(ang), jnp.sin(ang)
        reps = LANES // dim
        return (jnp.tile(jnp.concatenate([cos, cos], axis=1), (1, reps)),
                jnp.tile(jnp.concatenate([-sin, sin], axis=1), (1, reps)))

    ca, sa = tab(HEAD_DIM)
    ci, si = tab(IDX_DIM)
    return ca, sa, ci, si


def _pack_w_in(w):
    d_blk = N_HEADS * HEAD_DIM
    n_idx = N_IDX_HEADS * IDX_DIM + IDX_DIM + N_IDX_HEADS
    v_lo, v_hi = 5 * d_blk, 6 * d_blk
    main = jnp.concatenate([w[:, :v_lo], w[:, v_hi + n_idx:]], axis=1)
    idx = jnp.pad(w[:, v_hi:v_hi + n_idx], ((0, 0), (0, IDX_COLS - n_idx)))
    return main.astype(BF16), w[:, v_lo:v_hi].T.astype(BF16), idx.astype(BF16)


def kernel(x, mem, attn_norm, w_in, conv_w, q_norm, k_norm, mem_norm, w_mem_kv, mq_norm, mk_norm, w_br_conv, w_br_attn, w_br_mem, w_out, ffn_norm, ff_gate, ff_up, ff_down, router, ex_gate, ex_up, ex_down):
    batch, seq, d = x.shape
    depth = attn_norm.shape[0]
    n_mem = mem.shape[1]
    topk = min(MAX_TOPK, seq // 4)
    m = batch * seq
    rope = _rope_tables(seq)
    mem2d = mem.reshape(batch * n_mem, d)
    xc = x.reshape(m, d)
    row = lambda a: a.reshape(1, -1)
    for l in range(depth):
        w_main, w_vt, w_idx = _pack_w_in(w_in[l])
        g = row(attn_norm[l])
        p_main = norm_matmul(xc, g, w_main, tm=1024, tn=1024)
        p_idx = norm_matmul(xc, g, w_idx, tm=1024, tn=IDX_COLS)
        vt = norm_matmul_t(xc, g, w_vt, seq, tm=512)
        qn, kn, qip, kil, wis = dsa_prep(p_main, p_idx, rope, row(q_norm[l]), row(k_norm[l]), seq, ts=512)
        att = dsa_attention(qn, qip, wis, kil, kn, vt, _logits_bounded(q_norm[l], k_norm[l]),
                            batch, seq, topk)
        mk, mv = mem_kv(mem2d, row(mem_norm[l]), w_mem_kv[l].astype(BF16), row(mk_norm[l]), batch, n_mem)
        memo = mem_attention(p_main, mk, mv, row(mq_norm[l]), seq, n_mem, tm=512)
        xc = merge(xc, p_main, att, memo, conv_w[l], w_br_conv[l].astype(BF16), w_br_attn[l].astype(BF16),
                   w_br_mem[l].astype(BF16), w_out[l].astype(BF16), seq, tm=256)
        j = l // 2
        if l % 2 == 0:
            xc = ffn_dense(xc, row(ffn_norm[l]), ff_gate[j].astype(BF16), ff_up[j].astype(BF16),
                           ff_down[j].astype(BF16), tm=512, tf=512)
        else:
            xc = moe_layer(xc, row(ffn_norm[l]), router[j], ex_gate[j].astype(BF16), ex_up[j].astype(BF16),
                           ex_down[j].astype(BF16))
    return xc.reshape(batch, seq, d)
```

```python
import functools
import math

import numpy as np
import jax
import jax.numpy as jnp
from jax import lax
from jax.experimental import pallas as pl
from jax.experimental.pallas import tpu as pltpu

F32 = jnp.float32
BF16 = jnp.bfloat16
I32 = jnp.int32
I16 = jnp.int16

N_HEADS = 8
HEAD_DIM = 128
N_IDX_HEADS = 8
IDX_DIM = 64
MAX_TOPK = 256
N_MEM_HEADS = 4
MEM_HEAD_DIM = 256
N_EXPERTS = 8
TOP_K = 2
CONV_WIDTH = 3
ROPE_THETA = 10000.0
EPS = 1e-6

LANES = 128
SUBLANES = 8
INT_MIN = -(2 ** 31)
INT_MAX = 2 ** 31 - 1
HALF_BITS = 16
HALF_MAX = 2 ** 15 - 1
LOG2E = math.log2(math.e)
VMEM_LIMIT = 56 * 1024 * 1024

COL_CB, COL_CC, COL_CU, COL_Q, COL_K, COL_QM, COL_GC, COL_GA, COL_GM = range(9)
IDX_COLS = 640


def _cparams(*sem):
    return pltpu.CompilerParams(dimension_semantics=sem, vmem_limit_bytes=VMEM_LIMIT)


def _rms(x, g):
    return x * lax.rsqrt(jnp.mean(x * x, axis=-1, keepdims=True) + EPS) * g


def _norm_matmul_kernel(x_ref, g_ref, w_ref, o_ref, h_ref):
    @pl.when(pl.program_id(1) == 0)
    def _():
        h_ref[...] = _rms(x_ref[...], g_ref[...]).astype(BF16)

    o_ref[...] = jnp.dot(h_ref[...], w_ref[...], preferred_element_type=F32)


def norm_matmul(x, g, w, tm, tn):
    m, k = x.shape
    n = w.shape[1]
    return pl.pallas_call(
        _norm_matmul_kernel,
        grid=(m // tm, n // tn),
        in_specs=[pl.BlockSpec((tm, k), lambda i, j: (i, 0)),
                  pl.BlockSpec((1, k), lambda i, j: (0, 0)),
                  pl.BlockSpec((k, tn), lambda i, j: (0, j))],
        out_specs=pl.BlockSpec((tm, tn), lambda i, j: (i, j)),
        out_shape=jax.ShapeDtypeStruct((m, n), F32),
        scratch_shapes=[pltpu.VMEM((tm, k), BF16)],
        compiler_params=_cparams("parallel", "arbitrary"),
        name="norm_matmul",
    )(x, g, w)


def _norm_matmul_t_kernel(x_ref, g_ref, wt_ref, o_ref):
    h = _rms(x_ref[...], g_ref[...]).astype(BF16)
    o_ref[...] = lax.dot_general(wt_ref[...], h, (((1,), (1,)), ((), ())),
                                 preferred_element_type=F32).astype(o_ref.dtype)


def norm_matmul_t(x, g, wt, seq, tm):
    m, k = x.shape
    n = wt.shape[0]
    tiles_per_seq = seq // tm
    return pl.pallas_call(
        _norm_matmul_t_kernel,
        grid=(m // tm,),
        in_specs=[pl.BlockSpec((tm, k), lambda i: (i, 0)),
                  pl.BlockSpec((1, k), lambda i: (0, 0)),
                  pl.BlockSpec((n, k), lambda i: (0, 0))],
        out_specs=pl.BlockSpec((n, tm), lambda i: (i // tiles_per_seq, i % tiles_per_seq)),
        out_shape=jax.ShapeDtypeStruct((m // seq * n, seq), BF16),
        compiler_params=_cparams("parallel"),
        name="norm_matmul_t",
    )(x, g, wt)


def _prep_kernel(q_ref, k_ref, idx_ref, ca_ref, sa_ref, ci_ref, si_ref, qg_ref, kg_ref,
                 qn_ref, kn_ref, qip_ref, kil_ref, wis_ref, *, q_scale, w_scale):
    ca = ca_ref[...]
    sa = sa_ref[...]
    for h in range(N_HEADS):
        sl = slice(h * HEAD_DIM, (h + 1) * HEAD_DIM)
        for src, g_ref, dst, scale in ((q_ref, qg_ref, qn_ref, q_scale), (k_ref, kg_ref, kn_ref, 1.0)):
            n = _rms(src[:, sl], g_ref[...])
            o = n * ca + pltpu.roll(n, HEAD_DIM // 2, 1) * sa
            dst[:, sl] = (o * scale).astype(BF16)

    ci = ci_ref[...]
    si = si_ref[...]
    lane = lax.broadcasted_iota(I32, ci.shape, 1)
    first_half = (lane & (IDX_DIM // 2)) == 0
    low_head = lane < IDX_DIM

    def idx_rope(x):
        rot = jnp.where(first_half, pltpu.roll(x, LANES - IDX_DIM // 2, 1), pltpu.roll(x, IDX_DIM // 2, 1))
        return x * ci + rot * si

    for c in range(N_IDX_HEADS // 2):
        o = idx_rope(idx_ref[:, c * LANES:(c + 1) * LANES])
        qip_ref[:, (2 * c) * LANES:(2 * c + 1) * LANES] = jnp.where(low_head, o, 0.0).astype(BF16)
        qip_ref[:, (2 * c + 1) * LANES:(2 * c + 2) * LANES] = jnp.where(
            low_head, pltpu.roll(o, IDX_DIM, 1), 0.0).astype(BF16)
    tail = idx_ref[:, N_IDX_HEADS * IDX_DIM:N_IDX_HEADS * IDX_DIM + LANES]
    kil_ref[...] = jnp.where(low_head, idx_rope(tail), 0.0).astype(BF16)
    wis_ref[...] = tail * w_scale


def dsa_prep(p_main, p_idx, rope, q_norm, k_norm, seq, ts):
    m = p_main.shape[0]
    d = N_HEADS * HEAD_DIM
    n_seq_tiles = seq // ts
    ca, sa, ci, si = rope
    row = lambda c: pl.BlockSpec((ts, d), lambda i, c=c: (i, c))
    tab = pl.BlockSpec((ts, LANES), lambda i: (i % n_seq_tiles, 0))
    gain = pl.BlockSpec((1, HEAD_DIM), lambda i: (0, 0))
    out_d = pl.BlockSpec((ts, d), lambda i: (i, 0))
    out_l = pl.BlockSpec((ts, LANES), lambda i: (i, 0))
    q_scale = HEAD_DIM ** -0.5 * LOG2E
    w_scale = (N_IDX_HEADS * IDX_DIM) ** -0.5
    return pl.pallas_call(
        functools.partial(_prep_kernel, q_scale=q_scale, w_scale=w_scale),
        grid=(m // ts,),
        in_specs=[row(COL_Q), row(COL_K),
                  pl.BlockSpec((ts, IDX_COLS), lambda i: (i, 0)),
                  tab, tab, tab, tab, gain, gain],
        out_specs=[out_d, out_d, out_d, out_l, out_l],
        out_shape=[jax.ShapeDtypeStruct((m, d), BF16)] * 3
        + [jax.ShapeDtypeStruct((m, LANES), BF16), jax.ShapeDtypeStruct((m, LANES), F32)],
        compiler_params=_cparams("parallel"),
        name="dsa_prep",
    )(p_main, p_main, p_idx, ca, sa, ci, si, q_norm, k_norm)


SCORE_ROWS = 512
SLAB = 32
NT_DIMS = (((1,), (1,)), ((), ()))
ZERO_SCORE_BASE = 2 ** 23 - 1


def _dsa_kernel(qi_of, kb_of, last_of, nsc_of, nblk_of, nfull_of, bounded,
                qn_ref, qip_ref, wis_ref, kil_ref, kn_ref, vt_ref,
                o_ref,
                keys_ref, keys_hi_ref, thr_ref, w_ref, m_ref, l_ref, acc_ref, s_ref, gmax_ref,
                *, tq, tk, topk):
    p = pl.program_id(1)
    qi = qi_of[p]
    kb = kb_of[p]

    @pl.when(kb == 0)
    def _():
        w_ref[...] = wis_ref[...].T[IDX_DIM:IDX_DIM + N_IDX_HEADS, :]
        q_id = qi * tq + lax.broadcasted_iota(I32, (SCORE_ROWS, tq), 1)
        k_iota = lax.broadcasted_iota(I32, (SCORE_ROWS, tq), 0)
        n_sc = nsc_of[p]

        def score_chunk(c, on_diagonal):
            start = pl.multiple_of(c * SCORE_ROWS, SCORE_ROWS)
            kblk = kil_ref[pl.ds(start, SCORE_ROWS), :]
            sc = jnp.zeros((SCORE_ROWS, tq), F32)
            for h in range(N_IDX_HEADS):
                z = lax.dot_general(kblk, qip_ref[:, h * LANES:(h + 1) * LANES], NT_DIMS,
                                    preferred_element_type=F32)
                sc = sc + jnp.maximum(z, 0.0) * w_ref[h:h + 1, :]
            bits = pltpu.bitcast(sc, I32)
            key = jnp.where(bits < 0, bits ^ 0x7FFFFFFF, bits)
            k_id = k_iota + start
            key = jnp.where(key == 0, ZERO_SCORE_BASE - k_id, key)
            if on_diagonal:
                key = jnp.where(k_id <= q_id, key, INT_MIN)
            keys_ref[pl.ds(start, SCORE_ROWS), :] = key
            keys_hi_ref[pl.ds(start, SCORE_ROWS), :] = (key >> HALF_BITS).astype(I16)
            return key

        def score_block(t, carry, on_diagonal):
            gmax = gmax_ref[...]
            for u in range(tk // SCORE_ROWS):
                key = score_chunk(t * (tk // SCORE_ROWS) + u, on_diagonal)
                for j in range(SCORE_ROWS // MAX_TOPK):
                    gmax = jnp.maximum(gmax, key[j * MAX_TOPK:(j + 1) * MAX_TOPK, :])
            gmax_ref[...] = gmax
            return carry

        gmax_ref[...] = jnp.full(gmax_ref.shape, INT_MIN, I32)
        lax.fori_loop(0, nfull_of[p], functools.partial(score_block, on_diagonal=False), 0)
        lax.fori_loop(nfull_of[p], nblk_of[p], functools.partial(score_block, on_diagonal=True), 0)

        def count_ge(cand):
            def count_step(c, acc):
                start = pl.multiple_of(c * SCORE_ROWS, SCORE_ROWS)
                for j in range(SCORE_ROWS // SLAB):
                    acc = acc + (keys_ref[pl.ds(start + j * SLAB, SLAB), :] >= cand).astype(I32)
                return acc

            acc = lax.fori_loop(0, n_sc, count_step, jnp.zeros((SLAB, tq), I32))
            return jnp.sum(acc.astype(F32), axis=0, keepdims=True)

        gmax = gmax_ref[...]
        lo0 = jnp.maximum(jnp.min(gmax, axis=0, keepdims=True), INT_MIN + 1)
        hi0 = jnp.minimum(jnp.max(gmax, axis=0, keepdims=True), INT_MAX - 1) + 1

        def open_interval(lo, hi):
            return jnp.max(((lo + 1) < hi).astype(I32)) > 0

        def halve(lo, hi):
            mid = (lo & hi) + ((lo ^ hi) >> 1)
            cnt = count_ge(mid)
            keep = cnt >= topk
            lo = jnp.where(keep, mid, lo)
            hi = jnp.where(cnt == topk, mid + 1, jnp.where(keep, hi, mid))
            return lo, hi

        def bisect(state):
            lo, hi = halve(*halve(state[0], state[1]))
            return lo, hi, open_interval(lo, hi)

        def count_hi_ge(cand):
            cand16 = cand.astype(I16)

            def count_step(c, acc):
                start = pl.multiple_of(c * SCORE_ROWS, SCORE_ROWS)
                for j in range(SCORE_ROWS // SLAB):
                    acc = acc + (keys_hi_ref[pl.ds(start + j * SLAB, SLAB), :] >= cand16).astype(I16)
                return acc

            acc = lax.fori_loop(0, n_sc, count_step, jnp.zeros((SLAB, tq), I16))
            return jnp.sum(acc.astype(F32), axis=0, keepdims=True)

        def halve_hi(lo, hi, hit):
            mid = (lo + hi) >> 1
            cnt = count_hi_ge(mid)
            keep = cnt >= topk
            exact = jnp.logical_and(cnt == topk, (lo + 1) < hi)
            lo = jnp.where(keep, mid, lo)
            hi = jnp.where(exact, mid + 1, jnp.where(keep, hi, mid))
            return lo, hi, jnp.where(exact, 1, hit)

        def bisect_hi(state):
            lo, hi, hit = halve_hi(*halve_hi(state[0], state[1], state[2]))
            return lo, hi, hit, open_interval(lo, hi)

        lo_h0 = lo0 >> HALF_BITS
        hi_h0 = ((hi0 - 1) >> HALF_BITS) + 1
        lo_h, hi_h, hit, _ = lax.while_loop(
            lambda st: st[3], bisect_hi,
            (lo_h0, hi_h0, jnp.zeros((1, tq), I32), open_interval(lo_h0, hi_h0)))
        hit = hit > 0
        lo_c = jnp.maximum(lo0, lo_h << HALF_BITS)
        hi_c = jnp.where(hi_h > HALF_MAX, hi0, jnp.minimum(hi0, jnp.minimum(hi_h, HALF_MAX) << HALF_BITS))
        lo1 = jnp.where(hit, lo_h << HALF_BITS, lo_c)
        hi1 = jnp.where(hit, lo1 + 1, hi_c)

        lo, _, _ = lax.while_loop(lambda st: st[2], bisect, (lo1, hi1, open_interval(lo1, hi1)))
        thr_ref[...] = lo

        m_ref[...] = jnp.full(m_ref.shape, -jnp.inf, F32)
        l_ref[...] = jnp.zeros(l_ref.shape, F32)
        acc_ref[...] = jnp.zeros(acc_ref.shape, F32)

    def mask_bias():
        keys = keys_ref[pl.ds(pl.multiple_of(kb * tk, tk), tk), :]
        return jnp.where(keys >= thr_ref[...], 0.0, -jnp.inf).astype(F32)

    def masked_logits(h, bias):
        sl = slice(h * HEAD_DIM, (h + 1) * HEAD_DIM)
        return lax.dot_general(kn_ref[:, sl], qn_ref[:, sl], NT_DIMS, preferred_element_type=F32) + bias


    @pl.when(bounded[0] == 1)
    def _():
        bias = mask_bias()
        s_ref[0] = masked_logits(0, bias)
        for h in range(N_HEADS):
            sl = slice(h * HEAD_DIM, (h + 1) * HEAD_DIM)
            if h + 1 < N_HEADS:
                s_ref[(h + 1) % 2] = masked_logits(h + 1, bias)
            pr = jnp.exp2(s_ref[h % 2])
            l_ref[h:h + 1, :] = l_ref[h:h + 1, :] + jnp.sum(pr, axis=0, keepdims=True)
            acc_ref[sl, :] = acc_ref[sl, :] + jnp.dot(
                vt_ref[sl, :], pr.astype(BF16), preferred_element_type=F32)

    @pl.when(bounded[0] == 0)
    def _():
        bias = mask_bias()

        def logits(h):
            s = masked_logits(h, bias)
            s_ref[h % 2] = s
            return jnp.max(s, axis=0, keepdims=True)

        def accumulate(h, m_cur):
            sl = slice(h * HEAD_DIM, (h + 1) * HEAD_DIM)
            m_prev = m_ref[h:h + 1, :]
            m_new = jnp.maximum(m_prev, m_cur)
            m_safe = jnp.where(m_new == -jnp.inf, 0.0, m_new)
            alpha = jnp.exp2(m_prev - m_safe)
            pr = jnp.exp2(s_ref[h % 2] - m_safe)
            l_ref[h:h + 1, :] = alpha * l_ref[h:h + 1, :] + jnp.sum(pr, axis=0, keepdims=True)
            m_ref[h:h + 1, :] = m_new
            acc_ref[sl, :] = acc_ref[sl, :] * alpha + jnp.dot(
                vt_ref[sl, :], pr.astype(BF16), preferred_element_type=F32)

        m_cur = logits(0)
        for h in range(N_HEADS):
            m_next = logits(h + 1) if h + 1 < N_HEADS else None
            accumulate(h, m_cur)
            m_cur = m_next

    @pl.when(last_of[p] == 1)
    def _():
        for h in range(N_HEADS):
            sl = slice(h * HEAD_DIM, (h + 1) * HEAD_DIM)
            o_ref[:, sl] = (acc_ref[sl, :] / l_ref[h:h + 1, :]).T.astype(o_ref.dtype)


def _dsa_schedule(seq, tq, tk):
    assert tk % SCORE_ROWS == 0
    qi_of, kb_of, last_of, nsc_of, nblk_of, nfull_of = [], [], [], [], [], []
    for qi in range(seq // tq):
        last_kb = ((qi + 1) * tq - 1) // tk
        for kb in range(last_kb + 1):
            qi_of.append(qi)
            kb_of.append(kb)
            last_of.append(int(kb == last_kb))
            nsc_of.append(-(-((qi + 1) * tq) // SCORE_ROWS))
            nblk_of.append(last_kb + 1)
            nfull_of.append((qi * tq + 1) // tk)
    return tuple(np.asarray(a, np.int32) for a in (qi_of, kb_of, last_of, nsc_of, nblk_of, nfull_of))


def dsa_attention(qn, qip, wis, kil, kn, vt, bounded, batch, seq, topk, tq=256, tk=1024):
    d = N_HEADS * HEAD_DIM
    nq = seq // tq
    nk = seq // tk
    assert seq <= ZERO_SCORE_BASE and topk <= MAX_TOPK
    sched = _dsa_schedule(seq, tq, tk)
    n_steps = len(sched[0])
    qrow = lambda w: pl.BlockSpec((tq, w), lambda b, p, qi_of, kb_of, *_: (b * nq + qi_of[p], 0))
    grid_spec = pltpu.PrefetchScalarGridSpec(
        num_scalar_prefetch=7,
        grid=(batch, n_steps),
        in_specs=[qrow(d), qrow(d), qrow(LANES),
                  pl.BlockSpec((seq, LANES), lambda b, p, *_: (b, 0)),
                  pl.BlockSpec((tk, d), lambda b, p, qi_of, kb_of, *_: (b * nk + kb_of[p], 0)),
                  pl.BlockSpec((d, tk), lambda b, p, qi_of, kb_of, *_: (b, kb_of[p]))],
        out_specs=qrow(d),
        scratch_shapes=[pltpu.VMEM((seq, tq), I32),
                        pltpu.VMEM((seq, tq), I16),
                        pltpu.VMEM((1, tq), I32),
                        pltpu.VMEM((N_IDX_HEADS, tq), F32),
                        pltpu.VMEM((N_HEADS, tq), F32),
                        pltpu.VMEM((N_HEADS, tq), F32),
                        pltpu.VMEM((d, tq), F32),
                        pltpu.VMEM((2, tk, tq), F32),
                        pltpu.VMEM((MAX_TOPK, tq), I32)],
    )
    return pl.pallas_call(
        functools.partial(_dsa_kernel, tq=tq, tk=tk, topk=topk),
        grid_spec=grid_spec,
        out_shape=jax.ShapeDtypeStruct((batch * seq, d), BF16),
        compiler_params=_cparams("parallel", "arbitrary"),
        name="dsa_attention",
    )(*sched, bounded, qn, qip, wis, kil, kn, vt)


LOGIT_BOUND = 100.0


def _logits_bounded(q_gain, k_gain):
    bound = HEAD_DIM * jnp.max(jnp.abs(q_gain)) * jnp.max(jnp.abs(k_gain)) * (HEAD_DIM ** -0.5 * LOG2E)
    return (bound <= LOGIT_BOUND).astype(I32).reshape(1)


def _mem_kv_kernel(mem_ref, g_ref, w_ref, kg_ref, mk_ref, mv_ref):
    d = N_MEM_HEADS * MEM_HEAD_DIM
    h = _rms(mem_ref[...], g_ref[...]).astype(BF16)
    kv = jnp.dot(h, w_ref[...], preferred_element_type=F32)
    for hd in range(N_MEM_HEADS):
        sl = slice(hd * MEM_HEAD_DIM, (hd + 1) * MEM_HEAD_DIM)
        mk_ref[:, sl] = _rms(kv[:, sl], kg_ref[...]).astype(BF16)
    mv_ref[...] = kv[:, d:].astype(BF16)


def mem_kv(mem2d, mem_norm, w_mem_kv, mk_norm, batch, n_mem):
    dm = mem2d.shape[1]
    d = N_MEM_HEADS * MEM_HEAD_DIM
    blk = pl.BlockSpec((n_mem, d), lambda b: (b, 0))
    return pl.pallas_call(
        _mem_kv_kernel,
        grid=(batch,),
        in_specs=[pl.BlockSpec((n_mem, dm), lambda b: (b, 0)),
                  pl.BlockSpec((1, dm), lambda b: (0, 0)),
                  pl.BlockSpec((dm, 2 * d), lambda b: (0, 0)),
                  pl.BlockSpec((1, MEM_HEAD_DIM), lambda b: (0, 0))],
        out_specs=[blk, blk],
        out_shape=[jax.ShapeDtypeStruct((batch * n_mem, d), BF16)] * 2,
        compiler_params=_cparams("arbitrary"),
        name="mem_kv",
    )(mem2d, mem_norm, w_mem_kv, mk_norm)


def _mem_attn_kernel(q_ref, mk_ref, mv_ref, g_ref, o_ref, *, scale):
    for hd in range(N_MEM_HEADS):
        sl = slice(hd * MEM_HEAD_DIM, (hd + 1) * MEM_HEAD_DIM)
        qh = (_rms(q_ref[:, sl], g_ref[...]) * scale).astype(BF16)
        s = lax.dot_general(qh, mk_ref[:, sl], (((1,), (1,)), ((), ())), preferred_element_type=F32)
        pr = jnp.exp2(s - jnp.max(s, axis=1, keepdims=True))
        o = jnp.dot(pr.astype(BF16), mv_ref[:, sl], preferred_element_type=F32)
        o_ref[:, sl] = (o / jnp.sum(pr, axis=1, keepdims=True)).astype(o_ref.dtype)


def mem_attention(p_main, mk, mv, mq_norm, seq, n_mem, tm):
    m = p_main.shape[0]
    d = N_MEM_HEADS * MEM_HEAD_DIM
    tiles_per_seq = seq // tm
    kv = pl.BlockSpec((n_mem, d), lambda i: (i // tiles_per_seq, 0))
    return pl.pallas_call(
        functools.partial(_mem_attn_kernel, scale=MEM_HEAD_DIM ** -0.5 * LOG2E),
        grid=(m // tm,),
        in_specs=[pl.BlockSpec((tm, d), lambda i: (i, COL_QM)), kv, kv,
                  pl.BlockSpec((1, MEM_HEAD_DIM), lambda i: (0, 0))],
        out_specs=pl.BlockSpec((tm, d), lambda i: (i, 0)),
        out_shape=jax.ShapeDtypeStruct((m, d), BF16),
        compiler_params=_cparams("parallel"),
        name="mem_attention",
    )(p_main, mk, mv, mq_norm)


HALO = 8


def _merge_kernel(x_ref, cb_ref, cc_ref, cu_ref, cch_ref, cuh_ref, gc_ref, ga_ref, gm_ref,
                  att_ref, mem_ref, cw_ref, wc_ref, wa_ref, wm_ref, wo_ref, o_ref, *, tiles_per_seq):
    i = pl.program_id(0)
    pcur = cc_ref[...] * cu_ref[...]
    halo = jnp.where(i % tiles_per_seq == 0, 0.0, cch_ref[...] * cuh_ref[...])
    row = lax.broadcasted_iota(I32, pcur.shape, 0)
    h1 = halo[HALO - 1:HALO, :]
    h2 = halo[HALO - 2:HALO - 1, :]
    p1 = jnp.where(row == 0, h1, pltpu.roll(pcur, 1, 0))
    p2 = jnp.where(row == 0, h2, jnp.where(row == 1, h1, pltpu.roll(pcur, 2, 0)))
    cw = cw_ref[...]
    conv = cb_ref[...] * (cw[0:1, :] * p2 + cw[1:2, :] * p1 + cw[2:3, :] * pcur)
    merged = jax.nn.sigmoid(gc_ref[...]) * jnp.dot(conv.astype(BF16), wc_ref[...], preferred_element_type=F32)
    merged += jax.nn.sigmoid(ga_ref[...]) * jnp.dot(att_ref[...], wa_ref[...], preferred_element_type=F32)
    merged += jax.nn.sigmoid(gm_ref[...]) * jnp.dot(mem_ref[...], wm_ref[...], preferred_element_type=F32)
    o_ref[...] = x_ref[...] + jnp.dot(merged.astype(BF16), wo_ref[...], preferred_element_type=F32)


def merge(x2d, p_main, att, memo, conv_w, w_c, w_a, w_m, w_o, seq, tm):
    m, d = x2d.shape
    tiles_per_seq = seq // tm
    col = lambda c: pl.BlockSpec((tm, d), lambda i, c=c: (i, c))
    halo = lambda c: pl.BlockSpec((HALO, d), lambda i, c=c: (jnp.maximum(i * (tm // HALO) - 1, 0), c))
    rowb = pl.BlockSpec((tm, d), lambda i: (i, 0))
    wspec = pl.BlockSpec((d, d), lambda i: (0, 0))
    return pl.pallas_call(
        functools.partial(_merge_kernel, tiles_per_seq=tiles_per_seq),
        grid=(m // tm,),
        in_specs=[rowb, col(COL_CB), col(COL_CC), col(COL_CU), halo(COL_CC), halo(COL_CU),
                  col(COL_GC), col(COL_GA), col(COL_GM), rowb, rowb,
                  pl.BlockSpec((CONV_WIDTH, d), lambda i: (0, 0)),
                  wspec, wspec, wspec, wspec],
        out_specs=rowb,
        out_shape=jax.ShapeDtypeStruct((m, d), F32),
        compiler_params=_cparams("parallel"),
        name="merge",
    )(x2d, p_main, p_main, p_main, p_main, p_main, p_main, p_main, p_main, att, memo,
      conv_w, w_c, w_a, w_m, w_o)


def _ffn_kernel(x_ref, g_ref, wg_ref, wu_ref, wd_ref, o_ref, h_ref, acc_ref):
    j = pl.program_id(1)

    @pl.when(j == 0)
    def _():
        h_ref[...] = _rms(x_ref[...], g_ref[...]).astype(BF16)
        acc_ref[...] = jnp.zeros(acc_ref.shape, F32)

    h = h_ref[...]
    a = jax.nn.silu(jnp.dot(h, wg_ref[...], preferred_element_type=F32)) * jnp.dot(
        h, wu_ref[...], preferred_element_type=F32)
    acc_ref[...] += jnp.dot(a.astype(BF16), wd_ref[...], preferred_element_type=F32)

    @pl.when(j == pl.num_programs(1) - 1)
    def _():
        o_ref[...] = x_ref[...] + acc_ref[...]


def ffn_dense(x2d, g, w_gate, w_up, w_down, tm, tf):
    m, d = x2d.shape
    ff = w_gate.shape[1]
    return pl.pallas_call(
        _ffn_kernel,
        grid=(m // tm, ff // tf),
        in_specs=[pl.BlockSpec((tm, d), lambda i, j: (i, 0)),
                  pl.BlockSpec((1, d), lambda i, j: (0, 0)),
                  pl.BlockSpec((d, tf), lambda i, j: (0, j)),
                  pl.BlockSpec((d, tf), lambda i, j: (0, j)),
                  pl.BlockSpec((tf, d), lambda i, j: (j, 0))],
        out_specs=pl.BlockSpec((tm, d), lambda i, j: (i, 0)),
        out_shape=jax.ShapeDtypeStruct((m, d), F32),
        scratch_shapes=[pltpu.VMEM((tm, d), BF16), pltpu.VMEM((tm, d), F32)],
        compiler_params=_cparams("parallel", "arbitrary"),
        name="ffn_dense",
    )(x2d, g, w_gate, w_up, w_down)


def _router_kernel(x_ref, g_ref, r_ref, h_ref, eid_ref, gw_ref):
    h = _rms(x_ref[...], g_ref[...])
    _rows_to_tiles(h_ref, h)
    logits = jnp.dot(h, r_ref[...], preferred_element_type=F32, precision=lax.Precision.HIGHEST)
    lane = lax.broadcasted_iota(I32, logits.shape, 1)
    lanef = lane.astype(F32)
    logits = jnp.where(lane < N_EXPERTS, logits, -jnp.inf)
    m1 = jnp.max(logits, axis=1, keepdims=True)
    i1 = jnp.min(jnp.where(logits == m1, lanef, float(LANES)), axis=1, keepdims=True)
    rest = jnp.where(lanef == i1, -jnp.inf, logits)
    m2 = jnp.max(rest, axis=1, keepdims=True)
    i2 = jnp.min(jnp.where(rest == m2, lanef, float(LANES)), axis=1, keepdims=True)
    e = jnp.exp(m2 - m1)
    g1 = 1.0 / (1.0 + e)
    eid_ref[...] = jnp.where(lane == 0, i1, jnp.where(lane == 1, i2, 0.0)).astype(I32)
    gw_ref[...] = jnp.where(lane == 0, g1, jnp.where(lane == 1, e * g1, 0.0))


def router(x2d, g, router_pad, tm):
    m, d = x2d.shape
    assert d == SUBLANES * LANES
    lane_out = pl.BlockSpec((tm, LANES), lambda i: (i, 0))
    return pl.pallas_call(
        _router_kernel,
        grid=(m // tm,),
        in_specs=[pl.BlockSpec((tm, d), lambda i: (i, 0)),
                  pl.BlockSpec((1, d), lambda i: (0, 0)),
                  pl.BlockSpec((d, LANES), lambda i: (0, 0))],
        out_specs=[pl.BlockSpec((tm * SUBLANES, LANES), lambda i: (i, 0)), lane_out, lane_out],
        out_shape=[jax.ShapeDtypeStruct((m * SUBLANES, LANES), F32),
                   jax.ShapeDtypeStruct((m, LANES), I32),
                   jax.ShapeDtypeStruct((m, LANES), F32)],
        compiler_params=_cparams("parallel"),
        name="router",
    )(x2d, g, router_pad)


GATHER_UNROLL = 8


def _rows_to_tiles(ref, val):
    n = val.shape[0]
    for c in range(SUBLANES):
        ref[pl.ds(c, n, stride=SUBLANES), :] = val[:, c * LANES:(c + 1) * LANES]


def _tiles_to_rows(ref, first, n):
    return jnp.concatenate(
        [ref[pl.ds(first * SUBLANES + c, n, stride=SUBLANES), :] for c in range(SUBLANES)], axis=1)


def _tile_copy(src_ref, src_tile, dst_ref, dst_tile, sem):
    return pltpu.make_async_copy(src_ref.at[pl.ds(pl.multiple_of(src_tile * SUBLANES, SUBLANES), SUBLANES)],
                                 dst_ref.at[pl.ds(pl.multiple_of(dst_tile * SUBLANES, SUBLANES), SUBLANES)], sem)


def _start_tile_gather(src_ref, idx_ref, dst_ref, sem, n_tiles):
    def trip(g, c):
        for u in range(GATHER_UNROLL):
            r = g * GATHER_UNROLL + u
            _tile_copy(src_ref, idx_ref[0, 0, r], dst_ref, r, sem).start()
        return c

    lax.fori_loop(0, n_tiles // GATHER_UNROLL, trip, 0)


def _wait_tile_gather(src_ref, dst_ref, sem, n_tiles):
    def trip(g, c):
        for u in range(GATHER_UNROLL):
            _tile_copy(src_ref, 0, dst_ref, g * GATHER_UNROLL + u, sem).wait()
        return c

    lax.fori_loop(0, n_tiles // GATHER_UNROLL, trip, 0)


def _expert_kernel(blk_exp, n_used, tok_ref, tok_next_ref, h_ref, wg_ref, wu_ref, wd_ref, y_ref,
                   xbuf, xb_ref, acc_ref, sem, *, mb):
    b = pl.program_id(0)
    j = pl.program_id(1)
    active = b < n_used[0]
    last = j == pl.num_programs(1) - 1
    slot = b % 2

    def gather(idx_ref, s):
        _start_tile_gather(h_ref, idx_ref, xbuf.at[s], sem.at[s], mb)

    @pl.when(jnp.logical_and(b == 0, j == 0))
    def _():
        gather(tok_ref, 0)

    @pl.when(jnp.logical_and(active, j == 0))
    def _():
        _wait_tile_gather(h_ref, xbuf.at[slot], sem.at[slot], mb)
        xb_ref[...] = _tiles_to_rows(xbuf.at[slot], 0, mb).astype(BF16)
        acc_ref[...] = jnp.zeros(acc_ref.shape, F32)

    @pl.when(jnp.logical_and(b + 1 < n_used[0], j == 1))
    def _():
        gather(tok_next_ref, 1 - slot)

    @pl.when(active)
    def _():
        x = xb_ref[...]
        a = jax.nn.silu(jnp.dot(x, wg_ref[0], preferred_element_type=F32)) * jnp.dot(
            x, wu_ref[0], preferred_element_type=F32)
        acc_ref[...] += jnp.dot(a.astype(BF16), wd_ref[0], preferred_element_type=F32)

    @pl.when(jnp.logical_and(active, last))
    def _():
        _rows_to_tiles(y_ref, acc_ref[...])

    @pl.when(jnp.logical_and(jnp.logical_not(active), last))
    def _():
        y_ref[...] = jnp.zeros(y_ref.shape, F32)


def expert_ffn(hf, slot_tok, blk_exp, n_used, ex_gate, ex_up, ex_down, mb, tf):
    d = hf.shape[1] * SUBLANES
    n_blk = blk_exp.shape[0]
    ff = ex_gate.shape[2]
    assert ff // tf >= 2
    tok3 = slot_tok.reshape(n_blk, 1, mb)
    grid_spec = pltpu.PrefetchScalarGridSpec(
        num_scalar_prefetch=2,
        grid=(n_blk, ff // tf),
        in_specs=[pl.BlockSpec((1, 1, mb), lambda b, j, be, nu: (b, 0, 0), memory_space=pltpu.SMEM),
                  pl.BlockSpec((1, 1, mb), lambda b, j, be, nu: (jnp.minimum(b + 1, n_blk - 1), 0, 0),
                               memory_space=pltpu.SMEM),
                  pl.BlockSpec(memory_space=pl.ANY),
                  pl.BlockSpec((1, d, tf), lambda b, j, be, nu: (be[b], 0, j)),
                  pl.BlockSpec((1, d, tf), lambda b, j, be, nu: (be[b], 0, j)),
                  pl.BlockSpec((1, tf, d), lambda b, j, be, nu: (be[b], j, 0))],
        out_specs=pl.BlockSpec((mb * SUBLANES, LANES), lambda b, j, be, nu: (b, 0)),
        scratch_shapes=[pltpu.VMEM((2, mb * SUBLANES, LANES), F32), pltpu.VMEM((mb, d), BF16),
                        pltpu.VMEM((mb, d), F32), pltpu.SemaphoreType.DMA((2,))],
    )
    return pl.pallas_call(
        functools.partial(_expert_kernel, mb=mb),
        grid_spec=grid_spec,
        out_shape=jax.ShapeDtypeStruct((n_blk * mb * SUBLANES, LANES), F32),
        compiler_params=_cparams("arbitrary", "arbitrary"),
        name="expert_ffn",
    )(blk_exp, n_used, tok3, tok3, hf, ex_gate, ex_up, ex_down)


def _combine_kernel(pos_ref, pos_next_ref, x_ref, gw_ref, ys_ref, o_ref, buf, sem, *, tm):
    i = pl.program_id(0)
    slot = i % 2
    n_rows = TOP_K * tm

    def gather(idx_ref, s):
        _start_tile_gather(ys_ref, idx_ref, buf.at[s], sem.at[s], n_rows)

    @pl.when(i == 0)
    def _():
        gather(pos_ref, 0)

    @pl.when(i + 1 < pl.num_programs(0))
    def _():
        gather(pos_next_ref, 1 - slot)

    _wait_tile_gather(ys_ref, buf.at[slot], sem.at[slot], n_rows)
    out = x_ref[...]
    gw = gw_ref[...]
    for kk in range(TOP_K):
        out = out + _tiles_to_rows(buf.at[slot], kk * tm, tm) * gw[:, kk:kk + 1]
    o_ref[...] = out


def moe_combine(x2d, gw, ys, pos, tm):
    m, d = x2d.shape
    n_tiles = m // tm
    return pl.pallas_call(
        functools.partial(_combine_kernel, tm=tm),
        grid=(n_tiles,),
        in_specs=[pl.BlockSpec((1, 1, TOP_K * tm), lambda i: (i, 0, 0), memory_space=pltpu.SMEM),
                  pl.BlockSpec((1, 1, TOP_K * tm), lambda i: (jnp.minimum(i + 1, n_tiles - 1), 0, 0),
                               memory_space=pltpu.SMEM),
                  pl.BlockSpec((tm, d), lambda i: (i, 0)),
                  pl.BlockSpec((tm, LANES), lambda i: (i, 0)),
                  pl.BlockSpec(memory_space=pl.ANY)],
        out_specs=pl.BlockSpec((tm, d), lambda i: (i, 0)),
        out_shape=jax.ShapeDtypeStruct((m, d), F32),
        scratch_shapes=[pltpu.VMEM((2, TOP_K * tm * SUBLANES, LANES), F32), pltpu.SemaphoreType.DMA((2,))],
        compiler_params=_cparams("arbitrary"),
        name="moe_combine",
    )(pos, pos, x2d, gw, ys)


def _moe_plan(eid, n_tok, mb):
    n_asg = n_tok * TOP_K
    n_blk = n_asg // mb + N_EXPERTS
    cap = n_blk * mb
    e = eid.reshape(-1)
    tok = jnp.repeat(jnp.arange(n_tok, dtype=I32), TOP_K)
    onehot = (e[:, None] == jnp.arange(N_EXPERTS, dtype=I32)[None, :]).astype(I32)
    rank = jnp.cumsum(onehot, axis=0) - onehot
    rank = jnp.sum(rank * onehot, axis=1)
    counts = jnp.sum(onehot, axis=0)
    padded = (counts + mb - 1) // mb * mb
    pad_end = jnp.cumsum(padded)
    pad_start = pad_end - padded
    dest = (pad_start[e] + rank).astype(I32)
    slot_tok = jnp.zeros((cap,), I32).at[dest].set(tok)
    blk_start = jnp.arange(n_blk, dtype=I32) * mb
    blk_exp = jnp.minimum(jnp.searchsorted(pad_end, blk_start, side='right'), N_EXPERTS - 1).astype(I32)
    n_used = (pad_end[-1] // mb).astype(I32).reshape(1)
    return slot_tok, blk_exp, n_used, dest.reshape(n_tok, TOP_K)


def moe_layer(x2d, g, router_w, ex_gate, ex_up, ex_down, mb=512, tf=1792, tm_route=512, tm_comb=256):
    n_tok, d = x2d.shape
    router_pad = jnp.zeros((d, LANES), F32).at[:, :N_EXPERTS].set(router_w)
    hf, eid, gw = router(x2d, g, router_pad, tm_route)
    slot_tok, blk_exp, n_used, dest = _moe_plan(eid[:, :TOP_K], n_tok, mb)
    ys = expert_ffn(hf, slot_tok, blk_exp, n_used, ex_gate, ex_up, ex_down, mb, tf)
    pos = dest.reshape(n_tok // tm_comb, tm_comb, TOP_K).transpose(0, 2, 1).reshape(
        n_tok // tm_comb, 1, TOP_K * tm_comb)
    return moe_combine(x2d, gw, ys, pos, tm_comb)


def _rope_tables(seq):
    def tab(dim):
        inv = 1.0 / (ROPE_THETA ** (jnp.arange(0, dim, 2, dtype=F32) / dim))
        ang = jnp.arange(seq, dtype=F32)[:, None] * inv[None, :]
        cos, sin = jnp.cos(ang), jnp.sin(ang)
        reps = LANES // dim
        return (jnp.tile(jnp.concatenate([cos, cos], axis=1), (1, reps)),
                jnp.tile(jnp.concatenate([-sin, sin], axis=1), (1, reps)))

    ca, sa = tab(HEAD_DIM)
    ci, si = tab(IDX_DIM)
    return ca, sa, ci, si


def _pack_w_in(w):
    d_blk = N_HEADS * HEAD_DIM
    n_idx = N_IDX_HEADS * IDX_DIM + IDX_DIM + N_IDX_HEADS
    v_lo, v_hi = 5 * d_blk, 6 * d_blk
    main = jnp.concatenate([w[:, :v_lo], w[:, v_hi + n_idx:]], axis=1)
    idx = jnp.pad(w[:, v_hi:v_hi + n_idx], ((0, 0), (0, IDX_COLS - n_idx)))
    return main.astype(BF16), w[:, v_lo:v_hi].T.astype(BF16), idx.astype(BF16)


def kernel(x, mem, attn_norm, w_in, conv_w, q_norm, k_norm, mem_norm, w_mem_kv, mq_norm, mk_norm, w_br_conv, w_br_attn, w_br_mem, w_out, ffn_norm, ff_gate, ff_up, ff_down, router, ex_gate, ex_up, ex_down):
    batch, seq, d = x.shape
    depth = attn_norm.shape[0]
    n_mem = mem.shape[1]
    topk = min(MAX_TOPK, seq // 4)
    m = batch * seq
    rope = _rope_tables(seq)
    mem2d = mem.reshape(batch * n_mem, d)
    xc = x.reshape(m, d)
    row = lambda a: a.reshape(1, -1)
    for l in range(depth):
        w_main, w_vt, w_idx = _pack_w_in(w_in[l])
        g = row(attn_norm[l])
        p_main = norm_matmul(xc, g, w_main, tm=1024, tn=1024)
        p_idx = norm_matmul(xc, g, w_idx, tm=1024, tn=IDX_COLS)
        vt = norm_matmul_t(xc, g, w_vt, seq, tm=512)
        qn, kn, qip, kil, wis = dsa_prep(p_main, p_idx, rope, row(q_norm[l]), row(k_norm[l]), seq, ts=512)
        att = dsa_attention(qn, qip, wis, kil, kn, vt, _logits_bounded(q_norm[l], k_norm[l]),
                            batch, seq, topk)
        mk, mv = mem_kv(mem2d, row(mem_norm[l]), w_mem_kv[l].astype(BF16), row(mk_norm[l]), batch, n_mem)
        memo = mem_attention(p_main, mk, mv, row(mq_norm[l]), seq, n_mem, tm=512)
        xc = merge(xc, p_main, att, memo, conv_w[l], w_br_conv[l].astype(BF16), w_br_attn[l].astype(BF16),
                   w_br_mem[l].astype(BF16), w_out[l].astype(BF16), seq, tm=256)
        j = l // 2
        if l % 2 == 0:
            xc = ffn_dense(xc, row(ffn_norm[l]), ff_gate[j].astype(BF16), ff_up[j].astype(BF16),
                           ff_down[j].astype(BF16), tm=512, tf=512)
        else:
            xc = moe_layer(xc, row(ffn_norm[l]), router[j], ex_gate[j].astype(BF16), ex_up[j].astype(BF16),
                           ex_down[j].astype(BF16))
    return xc.reshape(batch, seq, d)
```

```python
import functools
import math

import numpy as np
import jax
import jax.numpy as jnp
from jax import lax
from jax.experimental import pallas as pl
from jax.experimental.pallas import tpu as pltpu

F32 = jnp.float32
BF16 = jnp.bfloat16
I32 = jnp.int32
I16 = jnp.int16

N_HEADS = 8
HEAD_DIM = 128
N_IDX_HEADS = 8
IDX_DIM = 64
MAX_TOPK = 256
N_MEM_HEADS = 4
MEM_HEAD_DIM = 256
N_EXPERTS = 8
TOP_K = 2
CONV_WIDTH = 3
ROPE_THETA = 10000.0
EPS = 1e-6

LANES = 128
SUBLANES = 8
INT_MIN = -(2 ** 31)
INT_MAX = 2 ** 31 - 1
HALF_BITS = 16
HALF_MAX = 2 ** 15 - 1
LOG2E = math.log2(math.e)
VMEM_LIMIT = 56 * 1024 * 1024

COL_CB, COL_CC, COL_CU, COL_Q, COL_K, COL_QM, COL_GC, COL_GA, COL_GM = range(9)
IDX_COLS = 640


def _cparams(*sem):
    return pltpu.CompilerParams(dimension_semantics=sem, vmem_limit_bytes=VMEM_LIMIT)


def _rms(x, g):
    return x * lax.rsqrt(jnp.mean(x * x, axis=-1, keepdims=True) + EPS) * g


def _norm_matmul_kernel(x_ref, g_ref, w_ref, o_ref, h_ref):
    @pl.when(pl.program_id(1) == 0)
    def _():
        h_ref[...] = _rms(x_ref[...], g_ref[...]).astype(BF16)

    o_ref[...] = jnp.dot(h_ref[...], w_ref[...], preferred_element_type=F32)


def norm_matmul(x, g, w, tm, tn):
    m, k = x.shape
    n = w.shape[1]
    return pl.pallas_call(
        _norm_matmul_kernel,
        grid=(m // tm, n // tn),
        in_specs=[pl.BlockSpec((tm, k), lambda i, j: (i, 0)),
                  pl.BlockSpec((1, k), lambda i, j: (0, 0)),
                  pl.BlockSpec((k, tn), lambda i, j: (0, j))],
        out_specs=pl.BlockSpec((tm, tn), lambda i, j: (i, j)),
        out_shape=jax.ShapeDtypeStruct((m, n), F32),
        scratch_shapes=[pltpu.VMEM((tm, k), BF16)],
        compiler_params=_cparams("parallel", "arbitrary"),
        name="norm_matmul",
    )(x, g, w)


def _norm_matmul_t_kernel(x_ref, g_ref, wt_ref, o_ref):
    h = _rms(x_ref[...], g_ref[...]).astype(BF16)
    o_ref[...] = lax.dot_general(wt_ref[...], h, (((1,), (1,)), ((), ())),
                                 preferred_element_type=F32).astype(o_ref.dtype)


def norm_matmul_t(x, g, wt, seq, tm):
    m, k = x.shape
    n = wt.shape[0]
    tiles_per_seq = seq // tm
    return pl.pallas_call(
        _norm_matmul_t_kernel,
        grid=(m // tm,),
        in_specs=[pl.BlockSpec((tm, k), lambda i: (i, 0)),
                  pl.BlockSpec((1, k), lambda i: (0, 0)),
                  pl.BlockSpec((n, k), lambda i: (0, 0))],
        out_specs=pl.BlockSpec((n, tm), lambda i: (i // tiles_per_seq, i % tiles_per_seq)),
        out_shape=jax.ShapeDtypeStruct((m // seq * n, seq), BF16),
        compiler_params=_cparams("parallel"),
        name="norm_matmul_t",
    )(x, g, wt)


def _prep_kernel(q_ref, k_ref, idx_ref, ca_ref, sa_ref, ci_ref, si_ref, qg_ref, kg_ref,
                 qn_ref, kn_ref, qip_ref, kil_ref, wis_ref, *, q_scale, w_scale):
    ca = ca_ref[...]
    sa = sa_ref[...]
    for h in range(N_HEADS):
        sl = slice(h * HEAD_DIM, (h + 1) * HEAD_DIM)
        for src, g_ref, dst, scale in ((q_ref, qg_ref, qn_ref, q_scale), (k_ref, kg_ref, kn_ref, 1.0)):
            n = _rms(src[:, sl], g_ref[...])
            o = n * ca + pltpu.roll(n, HEAD_DIM // 2, 1) * sa
            dst[:, sl] = (o * scale).astype(BF16)

    ci = ci_ref[...]
    si = si_ref[...]
    lane = lax.broadcasted_iota(I32, ci.shape, 1)
    first_half = (lane & (IDX_DIM // 2)) == 0
    low_head = lane < IDX_DIM

    def idx_rope(x):
        rot = jnp.where(first_half, pltpu.roll(x, LANES - IDX_DIM // 2, 1), pltpu.roll(x, IDX_DIM // 2, 1))
        return x * ci + rot * si

    for c in range(N_IDX_HEADS // 2):
        o = idx_rope(idx_ref[:, c * LANES:(c + 1) * LANES])
        qip_ref[:, (2 * c) * LANES:(2 * c + 1) * LANES] = jnp.where(low_head, o, 0.0).astype(BF16)
        qip_ref[:, (2 * c + 1) * LANES:(2 * c + 2) * LANES] = jnp.where(
            low_head, pltpu.roll(o, IDX_DIM, 1), 0.0).astype(BF16)
    tail = idx_ref[:, N_IDX_HEADS * IDX_DIM:N_IDX_HEADS * IDX_DIM + LANES]
    kil_ref[...] = jnp.where(low_head, idx_rope(tail), 0.0).astype(BF16)
    wis_ref[...] = tail * w_scale


def dsa_prep(p_main, p_idx, rope, q_norm, k_norm, seq, ts):
    m = p_main.shape[0]
    d = N_HEADS * HEAD_DIM
    n_seq_tiles = seq // ts
    ca, sa, ci, si = rope
    row = lambda c: pl.BlockSpec((ts, d), lambda i, c=c: (i, c))
    tab = pl.BlockSpec((ts, LANES), lambda i: (i % n_seq_tiles, 0))
    gain = pl.BlockSpec((1, HEAD_DIM), lambda i: (0, 0))
    out_d = pl.BlockSpec((ts, d), lambda i: (i, 0))
    out_l = pl.BlockSpec((ts, LANES), lambda i: (i, 0))
    q_scale = HEAD_DIM ** -0.5 * LOG2E
    w_scale = (N_IDX_HEADS * IDX_DIM) ** -0.5
    return pl.pallas_call(
        functools.partial(_prep_kernel, q_scale=q_scale, w_scale=w_scale),
        grid=(m // ts,),
        in_specs=[row(COL_Q), row(COL_K),
                  pl.BlockSpec((ts, IDX_COLS), lambda i: (i, 0)),
                  tab, tab, tab, tab, gain, gain],
        out_specs=[out_d, out_d, out_d, out_l, out_l],
        out_shape=[jax.ShapeDtypeStruct((m, d), BF16)] * 3
        + [jax.ShapeDtypeStruct((m, LANES), BF16), jax.ShapeDtypeStruct((m, LANES), F32)],
        compiler_params=_cparams("parallel"),
        name="dsa_prep",
    )(p_main, p_main, p_idx, ca, sa, ci, si, q_norm, k_norm)


SCORE_ROWS = 512
SLAB = 32
NT_DIMS = (((1,), (1,)), ((), ()))
ZERO_SCORE_BASE = 2 ** 23 - 1


def _dsa_kernel(qi_of, kb_of, last_of, nsc_of, nblk_of, nfull_of, bounded,
                qn_ref, qip_ref, wis_ref, kil_ref, kn_ref, vt_ref,
                o_ref,
                keys_ref, keys_hi_ref, thr_ref, w_ref, m_ref, l_ref, acc_ref, s_ref, gmax_ref,
                *, tq, tk, topk):
    p = pl.program_id(1)
    qi = qi_of[p]
    kb = kb_of[p]

    @pl.when(kb == 0)
    def _():
        w_ref[...] = wis_ref[...].T[IDX_DIM:IDX_DIM + N_IDX_HEADS, :]
        q_id = qi * tq + lax.broadcasted_iota(I32, (SCORE_ROWS, tq), 1)
        k_iota = lax.broadcasted_iota(I32, (SCORE_ROWS, tq), 0)
        n_sc = nsc_of[p]

        def score_chunk(c, on_diagonal):
            start = pl.multiple_of(c * SCORE_ROWS, SCORE_ROWS)
            kblk = kil_ref[pl.ds(start, SCORE_ROWS), :]
            sc = jnp.zeros((SCORE_ROWS, tq), F32)
            for h in range(N_IDX_HEADS):
                z = lax.dot_general(kblk, qip_ref[:, h * LANES:(h + 1) * LANES], NT_DIMS,
                                    preferred_element_type=F32)
                sc = sc + jnp.maximum(z, 0.0) * w_ref[h:h + 1, :]
            bits = pltpu.bitcast(sc, I32)
            key = jnp.where(bits < 0, bits ^ 0x7FFFFFFF, bits)
            k_id = k_iota + start
            key = jnp.where(key == 0, ZERO_SCORE_BASE - k_id, key)
            if on_diagonal:
                key = jnp.where(k_id <= q_id, key, INT_MIN)
            keys_ref[pl.ds(start, SCORE_ROWS), :] = key
            keys_hi_ref[pl.ds(start, SCORE_ROWS), :] = (key >> HALF_BITS).astype(I16)
            return key

        def score_block(t, carry, on_diagonal):
            gmax = gmax_ref[...]
            for u in range(tk // SCORE_ROWS):
                key = score_chunk(t * (tk // SCORE_ROWS) + u, on_diagonal)
                for j in range(SCORE_ROWS // MAX_TOPK):
                    gmax = jnp.maximum(gmax, key[j * MAX_TOPK:(j + 1) * MAX_TOPK, :])
            gmax_ref[...] = gmax
            return carry

        gmax_ref[...] = jnp.full(gmax_ref.shape, INT_MIN, I32)
        lax.fori_loop(0, nfull_of[p], functools.partial(score_block, on_diagonal=False), 0)
        lax.fori_loop(nfull_of[p], nblk_of[p], functools.partial(score_block, on_diagonal=True), 0)

        gmax = gmax_ref[...]
        lo0 = jnp.maximum(jnp.min(gmax, axis=0, keepdims=True), INT_MIN + 1)
        hi0 = jnp.minimum(jnp.max(gmax, axis=0, keepdims=True), INT_MAX - 1) + 1

        def open_interval(lo, hi):
            return jnp.max(((lo + 1) < hi).astype(I32)) > 0

        def count_hi_ge(cand):
            cand16 = cand.astype(I16)

            def count_step(c, acc):
                start = pl.multiple_of(c * SCORE_ROWS, SCORE_ROWS)
                for j in range(SCORE_ROWS // SLAB):
                    acc = acc + (keys_hi_ref[pl.ds(start + j * SLAB, SLAB), :] >= cand16).astype(I16)
                return acc

            acc = lax.fori_loop(0, n_sc, count_step, jnp.zeros((SLAB, tq), I16))
            return jnp.sum(acc.astype(F32), axis=0, keepdims=True)

        def halve(lo, hi, hit, offset):
            mid = (lo + hi) >> 1
            cnt = count_hi_ge(mid - offset)
            keep = cnt >= topk
            exact = jnp.logical_and(cnt == topk, (lo + 1) < hi)
            lo = jnp.where(keep, mid, lo)
            hi = jnp.where(exact, mid + 1, jnp.where(keep, hi, mid))
            return lo, hi, jnp.where(exact, 1, hit)

        def bisect(state, offset):
            lo, hi, hit = halve(*halve(state[0], state[1], state[2], offset), offset)
            return lo, hi, hit, open_interval(lo, hi)

        def search(lo, hi, offset):
            lo, hi, hit, _ = lax.while_loop(lambda st: st[3], functools.partial(bisect, offset=offset),
                                            (lo, hi, jnp.zeros((1, tq), I32), open_interval(lo, hi)))
            return lo, hi, hit > 0

        lo_h, hi_h, hit_h = search(lo0 >> HALF_BITS, ((hi0 - 1) >> HALF_BITS) + 1, 0)
        base = lo_h << HALF_BITS
        lo_c = jnp.maximum(lo0, base)
        hi_c = jnp.where(hi_h > HALF_MAX, hi0, jnp.minimum(hi0, jnp.minimum(hi_h, HALF_MAX) << HALF_BITS))

        win_hi = lo_h.astype(I16)

        def remap_step(c, carry):
            start = pl.multiple_of(c * SCORE_ROWS, SCORE_ROWS)
            for j in range(SCORE_ROWS // SLAB):
                rows = pl.ds(start + j * SLAB, SLAB)
                hi_half = keys_hi_ref[rows, :]
                low = ((keys_ref[rows, :] & (2 ** HALF_BITS - 1)) - (HALF_MAX + 1)).astype(I16)
                side = jnp.where(hi_half > win_hi, jnp.int16(HALF_MAX), jnp.int16(-HALF_MAX - 1))
                keys_hi_ref[rows, :] = jnp.where(hi_half == win_hi, low, side)
            return carry

        lax.fori_loop(0, n_sc, remap_step, 0)
        x_lo = jnp.where(hit_h, 0, lo_c - base)
        x_hi = jnp.where(hit_h, 1, hi_c - base)
        x_lo, _, _ = search(x_lo, x_hi, HALF_MAX + 1)
        thr_ref[...] = jnp.where(hit_h, base, base + x_lo)

        m_ref[...] = jnp.full(m_ref.shape, -jnp.inf, F32)
        l_ref[...] = jnp.zeros(l_ref.shape, F32)
        acc_ref[...] = jnp.zeros(acc_ref.shape, F32)

    def mask_bias():
        keys = keys_ref[pl.ds(pl.multiple_of(kb * tk, tk), tk), :]
        return jnp.where(keys >= thr_ref[...], 0.0, -jnp.inf).astype(F32)

    def masked_logits(h, bias):
        sl = slice(h * HEAD_DIM, (h + 1) * HEAD_DIM)
        return lax.dot_general(kn_ref[:, sl], qn_ref[:, sl], NT_DIMS, preferred_element_type=F32) + bias


    @pl.when(bounded[0] == 1)
    def _():
        bias = mask_bias()
        s_ref[0] = masked_logits(0, bias)
        for h in range(N_HEADS):
            sl = slice(h * HEAD_DIM, (h + 1) * HEAD_DIM)
            if h + 1 < N_HEADS:
                s_ref[(h + 1) % 2] = masked_logits(h + 1, bias)
            pr = jnp.exp2(s_ref[h % 2])
            l_ref[h:h + 1, :] = l_ref[h:h + 1, :] + jnp.sum(pr, axis=0, keepdims=True)
            acc_ref[sl, :] = acc_ref[sl, :] + jnp.dot(
                vt_ref[sl, :], pr.astype(BF16), preferred_element_type=F32)

    @pl.when(bounded[0] == 0)
    def _():
        bias = mask_bias()

        def logits(h):
            s = masked_logits(h, bias)
            s_ref[h % 2] = s
            return jnp.max(s, axis=0, keepdims=True)

        def accumulate(h, m_cur):
            sl = slice(h * HEAD_DIM, (h + 1) * HEAD_DIM)
            m_prev = m_ref[h:h + 1, :]
            m_new = jnp.maximum(m_prev, m_cur)
            m_safe = jnp.where(m_new == -jnp.inf, 0.0, m_new)
            alpha = jnp.exp2(m_prev - m_safe)
            pr = jnp.exp2(s_ref[h % 2] - m_safe)
            l_ref[h:h + 1, :] = alpha * l_ref[h:h + 1, :] + jnp.sum(pr, axis=0, keepdims=True)
            m_ref[h:h + 1, :] = m_new
            acc_ref[sl, :] = acc_ref[sl, :] * alpha + jnp.dot(
                vt_ref[sl, :], pr.astype(BF16), preferred_element_type=F32)

        m_cur = logits(0)
        for h in range(N_HEADS):
            m_next = logits(h + 1) if h + 1 < N_HEADS else None
            accumulate(h, m_cur)
            m_cur = m_next

    @pl.when(last_of[p] == 1)
    def _():
        for h in range(N_HEADS):
            sl = slice(h * HEAD_DIM, (h + 1) * HEAD_DIM)
            o_ref[:, sl] = (acc_ref[sl, :] / l_ref[h:h + 1, :]).T.astype(o_ref.dtype)


def _dsa_schedule(seq, tq, tk):
    assert tk % SCORE_ROWS == 0
    qi_of, kb_of, last_of, nsc_of, nblk_of, nfull_of = [], [], [], [], [], []
    for qi in range(seq // tq):
        last_kb = ((qi + 1) * tq - 1) // tk
        for kb in range(last_kb + 1):
            qi_of.append(qi)
            kb_of.append(kb)
            last_of.append(int(kb == last_kb))
            nsc_of.append(-(-((qi + 1) * tq) // SCORE_ROWS))
            nblk_of.append(last_kb + 1)
            nfull_of.append((qi * tq + 1) // tk)
    return tuple(np.asarray(a, np.int32) for a in (qi_of, kb_of, last_of, nsc_of, nblk_of, nfull_of))


def dsa_attention(qn, qip, wis, kil, kn, vt, bounded, batch, seq, topk, tq=256, tk=1024):
    d = N_HEADS * HEAD_DIM
    nq = seq // tq
    nk = seq // tk
    assert seq <= ZERO_SCORE_BASE and topk <= MAX_TOPK
    sched = _dsa_schedule(seq, tq, tk)
    n_steps = len(sched[0])
    qrow = lambda w: pl.BlockSpec((tq, w), lambda b, p, qi_of, kb_of, *_: (b * nq + qi_of[p], 0))
    grid_spec = pltpu.PrefetchScalarGridSpec(
        num_scalar_prefetch=7,
        grid=(batch, n_steps),
        in_specs=[qrow(d), qrow(d), qrow(LANES),
                  pl.BlockSpec((seq, LANES), lambda b, p, *_: (b, 0)),
                  pl.BlockSpec((tk, d), lambda b, p, qi_of, kb_of, *_: (b * nk + kb_of[p], 0)),
                  pl.BlockSpec((d, tk), lambda b, p, qi_of, kb_of, *_: (b, kb_of[p]))],
        out_specs=qrow(d),
        scratch_shapes=[pltpu.VMEM((seq, tq), I32),
                        pltpu.VMEM((seq, tq), I16),
                        pltpu.VMEM((1, tq), I32),
                        pltpu.VMEM((N_IDX_HEADS, tq), F32),
                        pltpu.VMEM((N_HEADS, tq), F32),
                        pltpu.VMEM((N_HEADS, tq), F32),
                        pltpu.VMEM((d, tq), F32),
                        pltpu.VMEM((2, tk, tq), F32),
                        pltpu.VMEM((MAX_TOPK, tq), I32)],
    )
    return pl.pallas_call(
        functools.partial(_dsa_kernel, tq=tq, tk=tk, topk=topk),
        grid_spec=grid_spec,
        out_shape=jax.ShapeDtypeStruct((batch * seq, d), BF16),
        compiler_params=_cparams("parallel", "arbitrary"),
        name="dsa_attention",
    )(*sched, bounded, qn, qip, wis, kil, kn, vt)


LOGIT_BOUND = 100.0


def _logits_bounded(q_gain, k_gain):
    bound = HEAD_DIM * jnp.max(jnp.abs(q_gain)) * jnp.max(jnp.abs(k_gain)) * (HEAD_DIM ** -0.5 * LOG2E)
    return (bound <= LOGIT_BOUND).astype(I32).reshape(1)


def _mem_kv_kernel(mem_ref, g_ref, w_ref, kg_ref, mk_ref, mv_ref):
    d = N_MEM_HEADS * MEM_HEAD_DIM
    h = _rms(mem_ref[...], g_ref[...]).astype(BF16)
    kv = jnp.dot(h, w_ref[...], preferred_element_type=F32)
    for hd in range(N_MEM_HEADS):
        sl = slice(hd * MEM_HEAD_DIM, (hd + 1) * MEM_HEAD_DIM)
        mk_ref[:, sl] = _rms(kv[:, sl], kg_ref[...]).astype(BF16)
    mv_ref[...] = kv[:, d:].astype(BF16)


def mem_kv(mem2d, mem_norm, w_mem_kv, mk_norm, batch, n_mem):
    dm = mem2d.shape[1]
    d = N_MEM_HEADS * MEM_HEAD_DIM
    blk = pl.BlockSpec((n_mem, d), lambda b: (b, 0))
    return pl.pallas_call(
        _mem_kv_kernel,
        grid=(batch,),
        in_specs=[pl.BlockSpec((n_mem, dm), lambda b: (b, 0)),
                  pl.BlockSpec((1, dm), lambda b: (0, 0)),
                  pl.BlockSpec((dm, 2 * d), lambda b: (0, 0)),
                  pl.BlockSpec((1, MEM_HEAD_DIM), lambda b: (0, 0))],
        out_specs=[blk, blk],
        out_shape=[jax.ShapeDtypeStruct((batch * n_mem, d), BF16)] * 2,
        compiler_params=_cparams("arbitrary"),
        name="mem_kv",
    )(mem2d, mem_norm, w_mem_kv, mk_norm)


def _mem_attn_kernel(q_ref, mk_ref, mv_ref, g_ref, o_ref, *, scale):
    for hd in range(N_MEM_HEADS):
        sl = slice(hd * MEM_HEAD_DIM, (hd + 1) * MEM_HEAD_DIM)
        qh = (_rms(q_ref[:, sl], g_ref[...]) * scale).astype(BF16)
        s = lax.dot_general(qh, mk_ref[:, sl], (((1,), (1,)), ((), ())), preferred_element_type=F32)
        pr = jnp.exp2(s - jnp.max(s, axis=1, keepdims=True))
        o = jnp.dot(pr.astype(BF16), mv_ref[:, sl], preferred_element_type=F32)
        o_ref[:, sl] = (o / jnp.sum(pr, axis=1, keepdims=True)).astype(o_ref.dtype)


def mem_attention(p_main, mk, mv, mq_norm, seq, n_mem, tm):
    m = p_main.shape[0]
    d = N_MEM_HEADS * MEM_HEAD_DIM
    tiles_per_seq = seq // tm
    kv = pl.BlockSpec((n_mem, d), lambda i: (i // tiles_per_seq, 0))
    return pl.pallas_call(
        functools.partial(_mem_attn_kernel, scale=MEM_HEAD_DIM ** -0.5 * LOG2E),
        grid=(m // tm,),
        in_specs=[pl.BlockSpec((tm, d), lambda i: (i, COL_QM)), kv, kv,
                  pl.BlockSpec((1, MEM_HEAD_DIM), lambda i: (0, 0))],
        out_specs=pl.BlockSpec((tm, d), lambda i: (i, 0)),
        out_shape=jax.ShapeDtypeStruct((m, d), BF16),
        compiler_params=_cparams("parallel"),
        name="mem_attention",
    )(p_main, mk, mv, mq_norm)


HALO = 8


def _merge_kernel(x_ref, cb_ref, cc_ref, cu_ref, cch_ref, cuh_ref, gc_ref, ga_ref, gm_ref,
                  att_ref, mem_ref, cw_ref, wc_ref, wa_ref, wm_ref, wo_ref, o_ref, *, tiles_per_seq):
    i = pl.program_id(0)
    pcur = cc_ref[...] * cu_ref[...]
    halo = jnp.where(i % tiles_per_seq == 0, 0.0, cch_ref[...] * cuh_ref[...])
    row = lax.broadcasted_iota(I32, pcur.shape, 0)
    h1 = halo[HALO - 1:HALO, :]
    h2 = halo[HALO - 2:HALO - 1, :]
    p1 = jnp.where(row == 0, h1, pltpu.roll(pcur, 1, 0))
    p2 = jnp.where(row == 0, h2, jnp.where(row == 1, h1, pltpu.roll(pcur, 2, 0)))
    cw = cw_ref[...]
    conv = cb_ref[...] * (cw[0:1, :] * p2 + cw[1:2, :] * p1 + cw[2:3, :] * pcur)
    merged = jax.nn.sigmoid(gc_ref[...]) * jnp.dot(conv.astype(BF16), wc_ref[...], preferred_element_type=F32)
    merged += jax.nn.sigmoid(ga_ref[...]) * jnp.dot(att_ref[...], wa_ref[...], preferred_element_type=F32)
    merged += jax.nn.sigmoid(gm_ref[...]) * jnp.dot(mem_ref[...], wm_ref[...], preferred_element_type=F32)
    o_ref[...] = x_ref[...] + jnp.dot(merged.astype(BF16), wo_ref[...], preferred_element_type=F32)


def merge(x2d, p_main, att, memo, conv_w, w_c, w_a, w_m, w_o, seq, tm):
    m, d = x2d.shape
    tiles_per_seq = seq // tm
    col = lambda c: pl.BlockSpec((tm, d), lambda i, c=c: (i, c))
    halo = lambda c: pl.BlockSpec((HALO, d), lambda i, c=c: (jnp.maximum(i * (tm // HALO) - 1, 0), c))
    rowb = pl.BlockSpec((tm, d), lambda i: (i, 0))
    wspec = pl.BlockSpec((d, d), lambda i: (0, 0))
    return pl.pallas_call(
        functools.partial(_merge_kernel, tiles_per_seq=tiles_per_seq),
        grid=(m // tm,),
        in_specs=[rowb, col(COL_CB), col(COL_CC), col(COL_CU), halo(COL_CC), halo(COL_CU),
                  col(COL_GC), col(COL_GA), col(COL_GM), rowb, rowb,
                  pl.BlockSpec((CONV_WIDTH, d), lambda i: (0, 0)),
                  wspec, wspec, wspec, wspec],
        out_specs=rowb,
        out_shape=jax.ShapeDtypeStruct((m, d), F32),
        compiler_params=_cparams("parallel"),
        name="merge",
    )(x2d, p_main, p_main, p_main, p_main, p_main, p_main, p_main, p_main, att, memo,
      conv_w, w_c, w_a, w_m, w_o)


def _ffn_kernel(x_ref, g_ref, wg_ref, wu_ref, wd_ref, o_ref, h_ref, acc_ref):
    j = pl.program_id(1)

    @pl.when(j == 0)
    def _():
        h_ref[...] = _rms(x_ref[...], g_ref[...]).astype(BF16)
        acc_ref[...] = jnp.zeros(acc_ref.shape, F32)

    h = h_ref[...]
    a = jax.nn.silu(jnp.dot(h, wg_ref[...], preferred_element_type=F32)) * jnp.dot(
        h, wu_ref[...], preferred_element_type=F32)
    acc_ref[...] += jnp.dot(a.astype(BF16), wd_ref[...], preferred_element_type=F32)

    @pl.when(j == pl.num_programs(1) - 1)
    def _():
        o_ref[...] = x_ref[...] + acc_ref[...]


def ffn_dense(x2d, g, w_gate, w_up, w_down, tm, tf):
    m, d = x2d.shape
    ff = w_gate.shape[1]
    return pl.pallas_call(
        _ffn_kernel,
        grid=(m // tm, ff // tf),
        in_specs=[pl.BlockSpec((tm, d), lambda i, j: (i, 0)),
                  pl.BlockSpec((1, d), lambda i, j: (0, 0)),
                  pl.BlockSpec((d, tf), lambda i, j: (0, j)),
                  pl.BlockSpec((d, tf), lambda i, j: (0, j)),
                  pl.BlockSpec((tf, d), lambda i, j: (j, 0))],
        out_specs=pl.BlockSpec((tm, d), lambda i, j: (i, 0)),
        out_shape=jax.ShapeDtypeStruct((m, d), F32),
        scratch_shapes=[pltpu.VMEM((tm, d), BF16), pltpu.VMEM((tm, d), F32)],
        compiler_params=_cparams("parallel", "arbitrary"),
        name="ffn_dense",
    )(x2d, g, w_gate, w_up, w_down)


def _router_kernel(x_ref, g_ref, r_ref, h_ref, eid_ref, gw_ref):
    h = _rms(x_ref[...], g_ref[...])
    _rows_to_tiles(h_ref, h)
    logits = jnp.dot(h, r_ref[...], preferred_element_type=F32, precision=lax.Precision.HIGHEST)
    lane = lax.broadcasted_iota(I32, logits.shape, 1)
    lanef = lane.astype(F32)
    logits = jnp.where(lane < N_EXPERTS, logits, -jnp.inf)
    m1 = jnp.max(logits, axis=1, keepdims=True)
    i1 = jnp.min(jnp.where(logits == m1, lanef, float(LANES)), axis=1, keepdims=True)
    rest = jnp.where(lanef == i1, -jnp.inf, logits)
    m2 = jnp.max(rest, axis=1, keepdims=True)
    i2 = jnp.min(jnp.where(rest == m2, lanef, float(LANES)), axis=1, keepdims=True)
    e = jnp.exp(m2 - m1)
    g1 = 1.0 / (1.0 + e)
    eid_ref[...] = jnp.where(lane == 0, i1, jnp.where(lane == 1, i2, 0.0)).astype(I32)
    gw_ref[...] = jnp.where(lane == 0, g1, jnp.where(lane == 1, e * g1, 0.0))


def router(x2d, g, router_pad, tm):
    m, d = x2d.shape
    assert d == SUBLANES * LANES
    lane_out = pl.BlockSpec((tm, LANES), lambda i: (i, 0))
    return pl.pallas_call(
        _router_kernel,
        grid=(m // tm,),
        in_specs=[pl.BlockSpec((tm, d), lambda i: (i, 0)),
                  pl.BlockSpec((1, d), lambda i: (0, 0)),
                  pl.BlockSpec((d, LANES), lambda i: (0, 0))],
        out_specs=[pl.BlockSpec((tm * SUBLANES, LANES), lambda i: (i, 0)), lane_out, lane_out],
        out_shape=[jax.ShapeDtypeStruct((m * SUBLANES, LANES), F32),
                   jax.ShapeDtypeStruct((m, LANES), I32),
                   jax.ShapeDtypeStruct((m, LANES), F32)],
        compiler_params=_cparams("parallel"),
        name="router",
    )(x2d, g, router_pad)


GATHER_UNROLL = 8


def _rows_to_tiles(ref, val):
    n = val.shape[0]
    for c in range(SUBLANES):
        ref[pl.ds(c, n, stride=SUBLANES), :] = val[:, c * LANES:(c + 1) * LANES]


def _tiles_to_rows(ref, first, n):
    return jnp.concatenate(
        [ref[pl.ds(first * SUBLANES + c, n, stride=SUBLANES), :] for c in range(SUBLANES)], axis=1)


def _tile_copy(src_ref, src_tile, dst_ref, dst_tile, sem):
    return pltpu.make_async_copy(src_ref.at[pl.ds(pl.multiple_of(src_tile * SUBLANES, SUBLANES), SUBLANES)],
                                 dst_ref.at[pl.ds(pl.multiple_of(dst_tile * SUBLANES, SUBLANES), SUBLANES)], sem)


def _start_tile_gather(src_ref, idx_ref, dst_ref, sem, n_tiles):
    def trip(g, c):
        for u in range(GATHER_UNROLL):
            r = g * GATHER_UNROLL + u
            _tile_copy(src_ref, idx_ref[0, 0, r], dst_ref, r, sem).start()
        return c

    lax.fori_loop(0, n_tiles // GATHER_UNROLL, trip, 0)


def _wait_tile_gather(src_ref, dst_ref, sem, n_tiles):
    def trip(g, c):
        for u in range(GATHER_UNROLL):
            _tile_copy(src_ref, 0, dst_ref, g * GATHER_UNROLL + u, sem).wait()
        return c

    lax.fori_loop(0, n_tiles // GATHER_UNROLL, trip, 0)


def _expert_kernel(blk_exp, n_used, tok_ref, tok_next_ref, h_ref, wg_ref, wu_ref, wd_ref, y_ref,
                   xbuf, xb_ref, acc_ref, sem, *, mb):
    b = pl.program_id(0)
    j = pl.program_id(1)
    active = b < n_used[0]
    last = j == pl.num_programs(1) - 1
    slot = b % 2

    def gather(idx_ref, s):
        _start_tile_gather(h_ref, idx_ref, xbuf.at[s], sem.at[s], mb)

    @pl.when(jnp.logical_and(b == 0, j == 0))
    def _():
        gather(tok_ref, 0)

    @pl.when(jnp.logical_and(active, j == 0))
    def _():
        _wait_tile_gather(h_ref, xbuf.at[slot], sem.at[slot], mb)
        xb_ref[...] = _tiles_to_rows(xbuf.at[slot], 0, mb).astype(BF16)
        acc_ref[...] = jnp.zeros(acc_ref.shape, F32)

    @pl.when(jnp.logical_and(b + 1 < n_used[0], j == 1))
    def _():
        gather(tok_next_ref, 1 - slot)

    @pl.when(active)
    def _():
        x = xb_ref[...]
        a = jax.nn.silu(jnp.dot(x, wg_ref[0], preferred_element_type=F32)) * jnp.dot(
            x, wu_ref[0], preferred_element_type=F32)
        acc_ref[...] += jnp.dot(a.astype(BF16), wd_ref[0], preferred_element_type=F32)

    @pl.when(jnp.logical_and(active, last))
    def _():
        _rows_to_tiles(y_ref, acc_ref[...])

    @pl.when(jnp.logical_and(jnp.logical_not(active), last))
    def _():
        y_ref[...] = jnp.zeros(y_ref.shape, F32)


def expert_ffn(hf, slot_tok, blk_exp, n_used, ex_gate, ex_up, ex_down, mb, tf):
    d = hf.shape[1] * SUBLANES
    n_blk = blk_exp.shape[0]
    ff = ex_gate.shape[2]
    assert ff // tf >= 2
    tok3 = slot_tok.reshape(n_blk, 1, mb)
    grid_spec = pltpu.PrefetchScalarGridSpec(
        num_scalar_prefetch=2,
        grid=(n_blk, ff // tf),
        in_specs=[pl.BlockSpec((1, 1, mb), lambda b, j, be, nu: (b, 0, 0), memory_space=pltpu.SMEM),
                  pl.BlockSpec((1, 1, mb), lambda b, j, be, nu: (jnp.minimum(b + 1, n_blk - 1), 0, 0),
                               memory_space=pltpu.SMEM),
                  pl.BlockSpec(memory_space=pl.ANY),
                  pl.BlockSpec((1, d, tf), lambda b, j, be, nu: (be[b], 0, j)),
                  pl.BlockSpec((1, d, tf), lambda b, j, be, nu: (be[b], 0, j)),
                  pl.BlockSpec((1, tf, d), lambda b, j, be, nu: (be[b], j, 0))],
        out_specs=pl.BlockSpec((mb * SUBLANES, LANES), lambda b, j, be, nu: (b, 0)),
        scratch_shapes=[pltpu.VMEM((2, mb * SUBLANES, LANES), F32), pltpu.VMEM((mb, d), BF16),
                        pltpu.VMEM((mb, d), F32), pltpu.SemaphoreType.DMA((2,))],
    )
    return pl.pallas_call(
        functools.partial(_expert_kernel, mb=mb),
        grid_spec=grid_spec,
        out_shape=jax.ShapeDtypeStruct((n_blk * mb * SUBLANES, LANES), F32),
        compiler_params=_cparams("arbitrary", "arbitrary"),
        name="expert_ffn",
    )(blk_exp, n_used, tok3, tok3, hf, ex_gate, ex_up, ex_down)


def _combine_kernel(pos_ref, pos_next_ref, x_ref, gw_ref, ys_ref, o_ref, buf, sem, *, tm):
    i = pl.program_id(0)
    slot = i % 2
    n_rows = TOP_K * tm

    def gather(idx_ref, s):
        _start_tile_gather(ys_ref, idx_ref, buf.at[s], sem.at[s], n_rows)

    @pl.when(i == 0)
    def _():
        gather(pos_ref, 0)

    @pl.when(i + 1 < pl.num_programs(0))
    def _():
        gather(pos_next_ref, 1 - slot)

    _wait_tile_gather(ys_ref, buf.at[slot], sem.at[slot], n_rows)
    out = x_ref[...]
    gw = gw_ref[...]
    for kk in range(TOP_K):
        out = out + _tiles_to_rows(buf.at[slot], kk * tm, tm) * gw[:, kk:kk + 1]
    o_ref[...] = out


def moe_combine(x2d, gw, ys, pos, tm):
    m, d = x2d.shape
    n_tiles = m // tm
    return pl.pallas_call(
        functools.partial(_combine_kernel, tm=tm),
        grid=(n_tiles,),
        in_specs=[pl.BlockSpec((1, 1, TOP_K * tm), lambda i: (i, 0, 0), memory_space=pltpu.SMEM),
                  pl.BlockSpec((1, 1, TOP_K * tm), lambda i: (jnp.minimum(i + 1, n_tiles - 1), 0, 0),
                               memory_space=pltpu.SMEM),
                  pl.BlockSpec((tm, d), lambda i: (i, 0)),
                  pl.BlockSpec((tm, LANES), lambda i: (i, 0)),
                  pl.BlockSpec(memory_space=pl.ANY)],
        out_specs=pl.BlockSpec((tm, d), lambda i: (i, 0)),
        out_shape=jax.ShapeDtypeStruct((m, d), F32),
        scratch_shapes=[pltpu.VMEM((2, TOP_K * tm * SUBLANES, LANES), F32), pltpu.SemaphoreType.DMA((2,))],
        compiler_params=_cparams("arbitrary"),
        name="moe_combine",
    )(pos, pos, x2d, gw, ys)


def _moe_plan(eid, n_tok, mb):
    n_asg = n_tok * TOP_K
    n_blk = n_asg // mb + N_EXPERTS
    cap = n_blk * mb
    e = eid.reshape(-1)
    tok = jnp.repeat(jnp.arange(n_tok, dtype=I32), TOP_K)
    onehot = (e[:, None] == jnp.arange(N_EXPERTS, dtype=I32)[None, :]).astype(I32)
    rank = jnp.cumsum(onehot, axis=0) - onehot
    rank = jnp.sum(rank * onehot, axis=1)
    counts = jnp.sum(onehot, axis=0)
    padded = (counts + mb - 1) // mb * mb
    pad_end = jnp.cumsum(padded)
    pad_start = pad_end - padded
    dest = (pad_start[e] + rank).astype(I32)
    slot_tok = jnp.zeros((cap,), I32).at[dest].set(tok)
    blk_start = jnp.arange(n_blk, dtype=I32) * mb
    blk_exp = jnp.minimum(jnp.searchsorted(pad_end, blk_start, side='right'), N_EXPERTS - 1).astype(I32)
    n_used = (pad_end[-1] // mb).astype(I32).reshape(1)
    return slot_tok, blk_exp, n_used, dest.reshape(n_tok, TOP_K)


def moe_layer(x2d, g, router_w, ex_gate, ex_up, ex_down, mb=512, tf=1792, tm_route=512, tm_comb=256):
    n_tok, d = x2d.shape
    router_pad = jnp.zeros((d, LANES), F32).at[:, :N_EXPERTS].set(router_w)
    hf, eid, gw = router(x2d, g, router_pad, tm_route)
    slot_tok, blk_exp, n_used, dest = _moe_plan(eid[:, :TOP_K], n_tok, mb)
    ys = expert_ffn(hf, slot_tok, blk_exp, n_used, ex_gate, ex_up, ex_down, mb, tf)
    pos = dest.reshape(n_tok // tm_comb, tm_comb, TOP_K).transpose(0, 2, 1).reshape(
        n_tok // tm_comb, 1, TOP_K * tm_comb)
    return moe_combine(x2d, gw, ys, pos, tm_comb)


def _rope_tables(seq):
    def tab(dim):
        inv = 1.0 / (ROPE_THETA ** (jnp.arange(0, dim, 2, dtype=F32) / dim))
        ang = jnp.arange(seq, dtype=F32)[:, None] * inv[None, :]
        cos, sin = jnp.cos(ang), jnp.sin(ang)
        reps = LANES // dim
        return (jnp.tile(jnp.concatenate([cos, cos], axis=1), (1, reps)),
                jnp.tile(jnp.concatenate([-sin, sin], axis=1), (1, reps)))

    ca, sa = tab(HEAD_DIM)
    ci, si = tab(IDX_DIM)
    return ca, sa, ci, si


def _pack_w_in(w):
    d_blk = N_HEADS * HEAD_DIM
    n_idx = N_IDX_HEADS * IDX_DIM + IDX_DIM + N_IDX_HEADS
    v_lo, v_hi = 5 * d_blk, 6 * d_blk
    main = jnp.concatenate([w[:, :v_lo], w[:, v_hi + n_idx:]], axis=1)
    idx = jnp.pad(w[:, v_hi:v_hi + n_idx], ((0, 0), (0, IDX_COLS - n_idx)))
    return main.astype(BF16), w[:, v_lo:v_hi].T.astype(BF16), idx.astype(BF16)


def kernel(x, mem, attn_norm, w_in, conv_w, q_norm, k_norm, mem_norm, w_mem_kv, mq_norm, mk_norm, w_br_conv, w_br_attn, w_br_mem, w_out, ffn_norm, ff_gate, ff_up, ff_down, router, ex_gate, ex_up, ex_down):
    batch, seq, d = x.shape
    depth = attn_norm.shape[0]
    n_mem = mem.shape[1]
    topk = min(MAX_TOPK, seq // 4)
    m = batch * seq
    rope = _rope_tables(seq)
    mem2d = mem.reshape(batch * n_mem, d)
    xc = x.reshape(m, d)
    row = lambda a: a.reshape(1, -1)
    for l in range(depth):
        w_main, w_vt, w_idx = _pack_w_in(w_in[l])
        g = row(attn_norm[l])
        p_main = norm_matmul(xc, g, w_main, tm=1024, tn=1024)
        p_idx = norm_matmul(xc, g, w_idx, tm=1024, tn=IDX_COLS)
        vt = norm_matmul_t(xc, g, w_vt, seq, tm=512)
        qn, kn, qip, kil, wis = dsa_prep(p_main, p_idx, rope, row(q_norm[l]), row(k_norm[l]), seq, ts=512)
        att = dsa_attention(qn, qip, wis, kil, kn, vt, _logits_bounded(q_norm[l], k_norm[l]),
                            batch, seq, topk)
        mk, mv = mem_kv(mem2d, row(mem_norm[l]), w_mem_kv[l].astype(BF16), row(mk_norm[l]), batch, n_mem)
        memo = mem_attention(p_main, mk, mv, row(mq_norm[l]), seq, n_mem, tm=512)
        xc = merge(xc, p_main, att, memo, conv_w[l], w_br_conv[l].astype(BF16), w_br_attn[l].astype(BF16),
                   w_br_mem[l].astype(BF16), w_out[l].astype(BF16), seq, tm=256)
        j = l // 2
        if l % 2 == 0:
            xc = ffn_dense(xc, row(ffn_norm[l]), ff_gate[j].astype(BF16), ff_up[j].astype(BF16),
                           ff_down[j].astype(BF16), tm=512, tf=512)
        else:
            xc = moe_layer(xc, row(ffn_norm[l]), router[j], ex_gate[j].astype(BF16), ex_up[j].astype(BF16),
                           ex_down[j].astype(BF16))
    return xc.reshape(batch, seq, d)
```
